```python
import jax, jax.numpy as jnp
from jax import lax
import numpy as np

D_MODEL = 2048
BATCH = 2
SEQ = 8192
DEPTH = 1

A_HEADS = 16
A_HEAD_DIM = 128
IDX_HEADS = 16
IDX_DIM = 64
INDEX_TOPK_MAX = 256
B_HEADS = 16
Q_LORA = 512
KV_LORA = 512
QK_NOPE = 128
QK_ROPE = 64
V_HEAD = 128
ROPE_THETA = 10000.0
D_FF = 5632
CONV_W = 3
PLE_DIM = 256
N_BRANCH = 2
Q_BLOCK = 128
DEEPNORM_ALPHA = (2.0 * DEPTH) ** 0.25
DEEPNORM_BETA = (8.0 * DEPTH) ** -0.25
LN_EPS = 1e-5
RMS_EPS = 1e-6
NEG = -1e30

SPLITS = (
    A_HEADS * A_HEAD_DIM,
    A_HEAD_DIM,
    A_HEAD_DIM,
    IDX_HEADS * IDX_DIM,
    IDX_DIM,
    IDX_HEADS,
    Q_LORA,
    KV_LORA,
    QK_ROPE,
    N_BRANCH * D_MODEL,
)
D_IN = sum(SPLITS)

kernel_name = "hybrid_dsa_mla_convffn_deepnorm_block"


def layer_norm(x, g, b):
    xf = x.astype(jnp.float32)
    mu = xf.mean(-1, keepdims=True)
    var = jnp.square(xf - mu).mean(-1, keepdims=True)
    return ((xf - mu) * lax.rsqrt(var + LN_EPS) * g + b).astype(x.dtype)


def rms_norm(x, g):
    xf = x.astype(jnp.float32)
    y = xf * lax.rsqrt(jnp.mean(xf * xf, -1, keepdims=True) + RMS_EPS)
    return (y * g).astype(x.dtype)


def rope_tables(S, dtype):
    pos = jnp.arange(S, dtype=jnp.float32)
    inv_freq = jnp.power(ROPE_THETA, -jnp.arange(0, QK_ROPE, 2, dtype=jnp.float32) / QK_ROPE)
    ang = pos[:, None] * inv_freq[None, :]
    return jnp.cos(ang).astype(dtype), jnp.sin(ang).astype(dtype)


def apply_rope(x, cos, sin):
    x1, x2 = jnp.split(x, 2, axis=-1)
    return jnp.concatenate([x1 * cos - x2 * sin, x1 * sin + x2 * cos], axis=-1)


def alibi_slopes(n_heads):
    return jnp.exp2(-8.0 * jnp.arange(1, n_heads + 1, dtype=jnp.float32) / n_heads)


def to_blocks(a, nb):
    return a.reshape(a.shape[0], nb, Q_BLOCK, *a.shape[2:]).swapaxes(0, 1)


def dsa_attention(q, k, v, q_idx, k_idx, w_idx, topk):
    B, S, H, Dh = q.shape
    nb = S // Q_BLOCK
    slopes = alibi_slopes(H)
    key_pos = jnp.arange(S)
    gather = jax.vmap(lambda a, i: a[i])
    scale = Dh ** -0.5

    def block(args):
        qb, qib, wib, blk = args
        t = blk * Q_BLOCK + jnp.arange(Q_BLOCK)
        s = jax.nn.relu(jnp.einsum('bqhd,bsd->bqhs', qib, k_idx))
        score = jnp.einsum('bqhs,bqh->bqs', s, wib).astype(jnp.float32)
        causal = key_pos[None, :] <= t[:, None]
        score = jnp.where(causal[None], score, NEG)
        _, idx = lax.top_k(score, topk)
        k_sel = gather(k, idx)
        v_sel = gather(v, idx)
        dist = (t[None, :, None] - idx).astype(jnp.float32)
        logits = jnp.einsum('bqhd,bqkd->bqhk', qb, k_sel).astype(jnp.float32) * scale
        logits = logits - slopes[:, None] * dist[:, :, None, :]
        logits = jnp.where((dist >= 0.0)[:, :, None, :], logits, NEG)
        probs = jax.nn.softmax(logits, axis=-1).astype(v.dtype)
        return jnp.einsum('bqhk,bqkd->bqhd', probs, v_sel)

    out = lax.map(block, (to_blocks(q, nb), to_blocks(q_idx, nb), to_blocks(w_idx, nb), jnp.arange(nb)))
    return out.swapaxes(0, 1).reshape(B, S, H * Dh)


def mla_attention(q_nope, q_pe, k_nope, k_pe, v):
    B, S, H, _ = q_nope.shape
    nb = S // Q_BLOCK
    key_pos = jnp.arange(S)
    scale = (QK_NOPE + QK_ROPE) ** -0.5

    def block(args):
        qn, qr, blk = args
        t = blk * Q_BLOCK + jnp.arange(Q_BLOCK)
        logits = (jnp.einsum('bqhd,bshd->bhqs', qn, k_nope)
                  + jnp.einsum('bqhr,bsr->bhqs', qr, k_pe)).astype(jnp.float32) * scale
        causal = key_pos[None, :] <= t[:, None]
        logits = jnp.where(causal[None, None], logits, NEG)
        probs = jax.nn.softmax(logits, axis=-1).astype(v.dtype)
        return jnp.einsum('bhqs,bshd->bqhd', probs, v)

    out = lax.map(block, (to_blocks(q_nope, nb), to_blocks(q_pe, nb), jnp.arange(nb)))
    return out.swapaxes(0, 1).reshape(B, S, H * V_HEAD)


def causal_dwconv(u, w, b):
    S = u.shape[1]
    up = jnp.pad(u, ((0, 0), (CONV_W - 1, 0), (0, 0)))
    y = up[:, 0:S] * w[0]
    for j in range(1, CONV_W):
        y = y + up[:, j:j + S] * w[j]
    return y + b


def setup_inputs(seed: int = 0) -> dict:
    key = jax.random.key(seed)
    ks = jax.random.split(key, 24)
    f32 = jnp.float32
    nrm = lambda k, shape, fan_in, gain=1.0: jax.random.normal(k, shape, f32) * (gain * fan_in ** -0.5)
    L = DEPTH
    return {
        'x': jax.random.normal(ks[0], (BATCH, SEQ, D_MODEL), f32),
        'p': jax.random.normal(ks[1], (DEPTH, BATCH, SEQ, PLE_DIM), f32),
        'w_in': nrm(ks[2], (L, D_MODEL, D_IN), D_MODEL),
        'q_a_norm': 1.0 + 0.02 * jax.random.normal(ks[3], (L, Q_LORA), f32),
        'kv_a_norm': 1.0 + 0.02 * jax.random.normal(ks[4], (L, KV_LORA), f32),
        'w_uq': nrm(ks[5], (L, Q_LORA, B_HEADS, QK_NOPE + QK_ROPE), Q_LORA),
        'w_uk': nrm(ks[6], (L, KV_LORA, B_HEADS, QK_NOPE), KV_LORA),
        'w_uv': nrm(ks[7], (L, KV_LORA, B_HEADS, V_HEAD), KV_LORA),
        'w_branch_a': nrm(ks[8], (L, A_HEADS * A_HEAD_DIM, D_MODEL), A_HEADS * A_HEAD_DIM),
        'w_branch_b': nrm(ks[9], (L, B_HEADS * V_HEAD, D_MODEL), B_HEADS * V_HEAD),
        'w_o': nrm(ks[10], (L, D_MODEL, D_MODEL), D_MODEL, DEEPNORM_BETA),
        'ln1_g': 1.0 + 0.02 * jax.random.normal(ks[11], (L, D_MODEL), f32),
        'ln1_b': 0.02 * jax.random.normal(ks[12], (L, D_MODEL), f32),
        'w_up': nrm(ks[13], (L, D_MODEL, 2 * D_FF), D_MODEL),
        'conv_w': nrm(ks[14], (L, CONV_W, 2 * D_FF), CONV_W),
        'conv_b': 0.02 * jax.random.normal(ks[15], (L, 2 * D_FF), f32),
        'w_down': nrm(ks[16], (L, D_FF, D_MODEL), D_FF, DEEPNORM_BETA),
        'w_pg': nrm(ks[17], (L, D_MODEL, D_MODEL), D_MODEL),
        'b_pg': 0.02 * jax.random.normal(ks[18], (L, D_MODEL), f32),
        'w_pe': nrm(ks[19], (L, PLE_DIM, D_MODEL), PLE_DIM),
        'ln2_g': 1.0 + 0.02 * jax.random.normal(ks[20], (L, D_MODEL), f32),
        'ln2_b': 0.02 * jax.random.normal(ks[21], (L, D_MODEL), f32),
    }


def reference(x, p, w_in, q_a_norm, kv_a_norm, w_uq, w_uk, w_uv, w_branch_a, w_branch_b, w_o,
              ln1_g, ln1_b, w_up, conv_w, conv_b, w_down, w_pg, b_pg, w_pe, ln2_g, ln2_b):
    B, S, _ = x.shape
    topk = min(INDEX_TOPK_MAX, S // 4)
    cos, sin = rope_tables(S, x.dtype)
    split_points = np.cumsum(SPLITS)[:-1].tolist()
    for i in range(DEPTH):
        u = x @ w_in[i]
        (q_a, k_a, v_a, q_idx, k_idx, w_idx, c_q, c_kv, k_pe, gate_logits) = jnp.split(u, split_points, axis=-1)
        attn_a = dsa_attention(q_a.reshape(B, S, A_HEADS, A_HEAD_DIM), k_a, v_a,
                               q_idx.reshape(B, S, IDX_HEADS, IDX_DIM), k_idx, w_idx, topk)
        c_q = rms_norm(c_q, q_a_norm[i])
        q_b = jnp.einsum('bsc,chd->bshd', c_q, w_uq[i])
        q_nope, q_pe = q_b[..., :QK_NOPE], q_b[..., QK_NOPE:]
        q_pe = apply_rope(q_pe, cos[:, None, :], sin[:, None, :])
        k_pe = apply_rope(k_pe, cos, sin)
        c_kv = rms_norm(c_kv, kv_a_norm[i])
        k_nope = jnp.einsum('bsc,chd->bshd', c_kv, w_uk[i])
        v_b = jnp.einsum('bsc,chd->bshd', c_kv, w_uv[i])
        attn_b = mla_attention(q_nope, q_pe, k_nope, k_pe, v_b)
        gates = jax.nn.sigmoid(gate_logits.reshape(B, S, N_BRANCH, D_MODEL))
        merged = gates[:, :, 0] * (attn_a @ w_branch_a[i]) + gates[:, :, 1] * (attn_b @ w_branch_b[i])
        x = layer_norm(DEEPNORM_ALPHA * x + merged @ w_o[i], ln1_g[i], ln1_b[i])
        hid = causal_dwconv(x @ w_up[i], conv_w[i], conv_b[i])
        h_gate, h_val = hid[..., :D_FF], hid[..., D_FF:]
        ffn = (jax.nn.silu(h_gate) * h_val) @ w_down[i]
        ple = jax.nn.sigmoid(x @ w_pg[i] + b_pg[i]) * (p[i] @ w_pe[i])
        x = layer_norm(DEEPNORM_ALPHA * x + ffn + ple, ln2_g[i], ln2_b[i])
    return x
```

```python
import functools

import jax
import jax.numpy as jnp
import numpy as np
from jax import lax
from jax.experimental import pallas as pl
from jax.experimental.pallas import tpu as pltpu

F32 = jnp.float32
BF16 = jnp.bfloat16

D_MODEL = 2048
A_HEADS = 16
A_HEAD_DIM = 128
IDX_HEADS = 16
IDX_DIM = 64
INDEX_TOPK_MAX = 256
B_HEADS = 16
Q_LORA = 512
KV_LORA = 512
QK_NOPE = 128
QK_ROPE = 64
V_HEAD = 128
ROPE_THETA = 10000.0
D_FF = 5632
CONV_W = 3
PLE_DIM = 256
LN_EPS = 1e-5
RMS_EPS = 1e-6
NEG = -1e30
DEEPNORM_ALPHA = 2.0 ** 0.25

LANES = 128
QK_PAD = 256
VMEM_LIMIT = 56 * 1024 * 1024
INT_MIN = -2 ** 31
INT_MAX = 2 ** 31 - 1


def _cparams(sem):
    return pltpu.CompilerParams(dimension_semantics=sem, vmem_limit_bytes=VMEM_LIMIT)


def _dot(a, b):
    return jnp.dot(a, b, preferred_element_type=F32)


def _dot_nt(a, b):
    return lax.dot_general(a, b, (((1,), (1,)), ((), ())), preferred_element_type=F32)


def _layer_norm(y, g, b):
    mu = jnp.mean(y, axis=-1, keepdims=True)
    d = y - mu
    var = jnp.mean(d * d, axis=-1, keepdims=True)
    return d * lax.rsqrt(var + LN_EPS) * g + b


def _rms_norm(c, g):
    return c * lax.rsqrt(jnp.mean(c * c, axis=-1, keepdims=True) + RMS_EPS) * g


def _rope_lanes(pe, cos2, sin2):
    lane = lax.broadcasted_iota(jnp.int32, pe.shape, 1)
    swapped = jnp.where(lane < QK_ROPE // 2, pltpu.roll(pe, LANES - QK_ROPE // 2, 1),
                        pltpu.roll(pe, QK_ROPE // 2, 1))
    return pe * cos2 + swapped * sin2


def _mm_kernel(x_ref, w_ref, o_ref, *, scale, act):
    y = _dot(x_ref[...], w_ref[...])
    if scale != 1.0:
        y = y * scale
    if act == "sigmoid":
        y = jax.nn.sigmoid(y)
    o_ref[...] = y.astype(o_ref.dtype)


def _matmul(x, w, out_dtype, *, scale=1.0, act=None, tm=1024, tn=1024):
    m, k = x.shape
    n = w.shape[1]
    tn = min(tn, n)
    assert m % tm == 0 and n % tn == 0
    return pl.pallas_call(
        functools.partial(_mm_kernel, scale=scale, act=act),
        grid=(n // tn, m // tm),
        in_specs=[pl.BlockSpec((tm, k), lambda j, i: (i, 0)),
                  pl.BlockSpec((k, tn), lambda j, i: (0, j))],
        out_specs=pl.BlockSpec((tm, tn), lambda j, i: (i, j)),
        out_shape=jax.ShapeDtypeStruct((m, n), out_dtype),
        compiler_params=_cparams(("parallel", "parallel")),
        name="proj_mm",
    )(x, w)


def _mla_q_kernel(c_ref, g_ref, w_ref, cos_ref, sin_ref, o_ref, *, scale):
    y = _rms_norm(c_ref[...], g_ref[...]).astype(BF16)
    q = _dot(y, w_ref[...])
    cos2 = cos_ref[...]
    sin2 = sin_ref[...]
    for h in range(B_HEADS):
        base = h * QK_PAD
        o_ref[:, base:base + QK_NOPE] = (q[:, base:base + QK_NOPE] * scale).astype(BF16)
        pe = q[:, base + QK_NOPE:base + QK_PAD]
        o_ref[:, base + QK_NOPE:base + QK_PAD] = (_rope_lanes(pe, cos2, sin2) * scale).astype(BF16)


def _mla_kv_kernel(c_ref, kpe_ref, g_ref, wk_ref, wv_ref, cos_ref, sin_ref, k_ref, v_ref):
    y = _rms_norm(c_ref[...], g_ref[...]).astype(BF16)
    kn = _dot(y, wk_ref[...])
    v_ref[...] = _dot(y, wv_ref[...]).astype(BF16)
    kpe = _rope_lanes(kpe_ref[...], cos_ref[...], sin_ref[...]).astype(BF16)
    for h in range(B_HEADS):
        base = h * QK_PAD
        k_ref[:, base:base + QK_NOPE] = kn[:, h * QK_NOPE:(h + 1) * QK_NOPE].astype(BF16)
        k_ref[:, base + QK_NOPE:base + QK_PAD] = kpe


def _softmax_step(s, m_prev, l_prev):
    m_new = jnp.maximum(m_prev, jnp.max(s, axis=-1, keepdims=True))
    alpha = jnp.exp(m_prev - m_new)
    p = jnp.exp(s - m_new)
    l_new = alpha * l_prev + jnp.sum(p, axis=-1, keepdims=True)
    return p, alpha, m_new, l_new


def _mla_attn_kernel(q_ref, k_ref, v_ref, o_ref, m_ref, l_ref, acc_ref, *, tq):
    qi = pl.program_id(2)
    q = q_ref[0]
    m_ref[...] = jnp.full(m_ref.shape, NEG, F32)
    l_ref[...] = jnp.zeros(l_ref.shape, F32)
    acc_ref[...] = jnp.zeros(acc_ref.shape, F32)

    def step(c, masked):
        off = pl.multiple_of(c * tq, tq)
        s = _dot_nt(q, k_ref[0, pl.ds(off, tq), :])
        if masked:
            row = lax.broadcasted_iota(jnp.int32, s.shape, 0)
            col = lax.broadcasted_iota(jnp.int32, s.shape, 1)
            s = jnp.where(col <= row, s, NEG)
        p, alpha, m_new, l_new = _softmax_step(s, m_ref[...], l_ref[...])
        m_ref[...] = m_new
        l_ref[...] = l_new
        acc_ref[...] = alpha * acc_ref[...] + _dot(p.astype(BF16), v_ref[0, pl.ds(off, tq), :])

    def body(c, carry):
        step(c, False)
        return carry

    lax.fori_loop(0, qi, body, 0)
    step(qi, True)
    o_ref[0] = (acc_ref[...] / l_ref[...]).astype(o_ref.dtype)


def _monotone_key(x):
    bits = lax.bitcast_convert_type(x, jnp.int32)
    return bits ^ ((bits >> 31) & INT_MAX)


def _dsa_kernel(qa_ref, qi_ref, w_ref, kext_ref, v_ref, ke_ref, ko_ref, aug_ref, o_ref,
                key_ref, lhs_ref, qs_ref, wb_ref, m_ref, l_ref, acc_ref,
                *, q_blk, topk, chunk, group):
    qb = pl.program_id(1)
    n_chunks = (qb * q_blk + q_blk + chunk - 1) // chunk
    rep = chunk // LANES
    n_pairs = IDX_HEADS // 2

    for h in range(A_HEADS):
        lhs_ref[h * q_blk:(h + 1) * q_blk, 0:A_HEAD_DIM] = qa_ref[0, :, h * A_HEAD_DIM:(h + 1) * A_HEAD_DIM]
    lhs_ref[:, A_HEAD_DIM:2 * A_HEAD_DIM] = aug_ref[...]
    for j in range(n_pairs):
        qs_ref[j * q_blk:(j + 1) * q_blk, :] = qi_ref[0, :, j * LANES:(j + 1) * LANES]
    for h in range(IDX_HEADS):
        wb_ref[h] = jnp.broadcast_to(w_ref[0, :, h:h + 1], (q_blk, LANES))

    t_col = qb * q_blk + lax.broadcasted_iota(jnp.int32, (q_blk, 1), 0)

    def score_chunk(c, carry):
        off = pl.multiple_of(c * chunk, chunk)
        qs = qs_ref[...]
        se = _dot_nt(qs, ke_ref[0, pl.ds(off, chunk), :])
        so = _dot_nt(qs, ko_ref[0, pl.ds(off, chunk), :])
        acc = jnp.zeros((q_blk, chunk), F32)
        for j in range(n_pairs):
            we = jnp.concatenate([wb_ref[2 * j]] * rep, axis=1)
            wo = jnp.concatenate([wb_ref[2 * j + 1]] * rep, axis=1)
            acc = acc + we * jnp.maximum(se[j * q_blk:(j + 1) * q_blk], 0.0)
            acc = acc + wo * jnp.maximum(so[j * q_blk:(j + 1) * q_blk], 0.0)
        s_pos = off + lax.broadcasted_iota(jnp.int32, (q_blk, chunk), 1)
        key_ref[:, pl.ds(off, chunk)] = jnp.where(s_pos <= t_col, _monotone_key(acc), INT_MIN)
        return carry

    lax.fori_loop(0, n_chunks, score_chunk, 0)

    kf = float(topk)

    def count_ge(th):
        thb = jnp.broadcast_to(th, (q_blk, chunk))

        def body(c, cnt):
            off = pl.multiple_of(c * chunk, chunk)
            return cnt + jnp.where(key_ref[:, pl.ds(off, chunk)] >= thb, 1.0, 0.0)

        cnt = lax.fori_loop(0, n_chunks, body, jnp.zeros((q_blk, chunk), F32))
        return jnp.sum(cnt, axis=-1, keepdims=True)

    n_causal = (t_col + 1).astype(F32)
    few = n_causal <= kf
    lo0 = jnp.full((q_blk, 1), INT_MIN + 1, jnp.int32)
    hi0 = jnp.where(few, INT_MIN + 2, INT_MAX).astype(jnp.int32)
    cnt_lo0 = n_causal
    cnt_hi0 = jnp.zeros((q_blk, 1), F32)

    def active(lo, hi, cnt_lo):
        return jnp.logical_and(lo + 1 < hi, cnt_lo != kf)

    def bis_cond(st):
        it, lo, hi, cnt_lo, cnt_hi = st
        n_act = jnp.max(jnp.where(active(lo, hi, cnt_lo), 1, 0))
        return jnp.logical_and(it < 34, n_act > 0)

    def bis_body(st):
        it, lo, hi, cnt_lo, cnt_hi = st
        act = active(lo, hi, cnt_lo)
        mid = (lo & hi) + ((lo ^ hi) >> 1)
        c = count_ge(mid)
        ge = jnp.logical_and(act, c >= kf)
        lt = jnp.logical_and(act, c < kf)
        return (it + 1, jnp.where(ge, mid, lo), jnp.where(lt, mid, hi),
                jnp.where(ge, c, cnt_lo), jnp.where(lt, c, cnt_hi))

    _, v_key, _, cnt_lo, cnt_hi = lax.while_loop(
        bis_cond, bis_body, (jnp.int32(0), lo0, hi0, cnt_lo0, cnt_hi0))

    tied = jnp.logical_and(cnt_lo > kf, jnp.logical_not(few))
    need = kf - cnt_hi
    n_tied = jnp.max(jnp.where(tied, 1, 0))

    def count_tie_le(jj):
        vb = jnp.broadcast_to(v_key, (q_blk, chunk))
        jb = jnp.broadcast_to(jj, (q_blk, chunk))

        def body(c, cnt):
            off = pl.multiple_of(c * chunk, chunk)
            s_pos = off + lax.broadcasted_iota(jnp.int32, (q_blk, chunk), 1)
            hit = jnp.where(key_ref[:, pl.ds(off, chunk)] == vb, jnp.where(s_pos <= jb, 1.0, 0.0), 0.0)
            return cnt + hit

        cnt = lax.fori_loop(0, n_chunks, body, jnp.zeros((q_blk, chunk), F32))
        return jnp.sum(cnt, axis=-1, keepdims=True)

    def tie_search(_):
        def body(_, st):
            lo, hi = st
            mid = (lo + hi) >> 1
            ok = count_tie_le(mid) >= need
            upd = lo + 1 < hi
            return (jnp.where(jnp.logical_and(upd, jnp.logical_not(ok)), mid, lo),
                    jnp.where(jnp.logical_and(upd, ok), mid, hi))

        lo = jnp.full((q_blk, 1), -1, jnp.int32)
        hi = jnp.broadcast_to(t_col, (q_blk, 1))
        _, hi = lax.fori_loop(0, 14, body, (lo, hi))
        return jnp.where(tied, hi, INT_MAX)

    j_max = lax.cond(n_tied > 0, tie_search, lambda _: jnp.full((q_blk, 1), INT_MAX, jnp.int32), 0)

    m_ref[...] = jnp.full(m_ref.shape, NEG, F32)
    l_ref[...] = jnp.zeros(l_ref.shape, F32)
    acc_ref[...] = jnp.zeros(acc_ref.shape, F32)
    rows = group * q_blk
    n_groups = A_HEADS // group

    def attn_chunk(c, carry):
        off = pl.multiple_of(c * chunk, chunk)
        keys = key_ref[:, pl.ds(off, chunk)]
        s_pos = off + lax.broadcasted_iota(jnp.int32, (q_blk, chunk), 1)
        sel = jnp.where(keys > v_key, 1, jnp.where(keys == v_key, jnp.where(s_pos <= j_max, 1, 0), 0))
        bias = jnp.where(sel > 0, 0.0, NEG)
        kc = kext_ref[0, pl.ds(off, chunk), :]
        vc = v_ref[0, pl.ds(off, chunk), :]
        for g in range(n_groups):
            r0 = g * rows
            s = _dot_nt(lhs_ref[r0:r0 + rows, :], kc)
            s = (s.reshape(group, q_blk, chunk) + bias[None]).reshape(rows, chunk)
            p, alpha, m_new, l_new = _softmax_step(s, m_ref[r0:r0 + rows, :], l_ref[r0:r0 + rows, :])
            m_ref[r0:r0 + rows, :] = m_new
            l_ref[r0:r0 + rows, :] = l_new
            acc_ref[r0:r0 + rows, :] = alpha * acc_ref[r0:r0 + rows, :] + _dot(p.astype(BF16), vc)
        return carry

    lax.fori_loop(0, n_chunks, attn_chunk, 0)
    for h in range(A_HEADS):
        r0 = h * q_blk
        o_ref[0, :, h * A_HEAD_DIM:(h + 1) * A_HEAD_DIM] = (
            acc_ref[r0:r0 + q_blk, :] / l_ref[r0:r0 + q_blk, :]).astype(o_ref.dtype)


def _merge_kernel(a_ref, b_ref, wa_ref, wb_ref, g0_ref, g1_ref, o_ref):
    ya = _dot(a_ref[...], wa_ref[...])
    yb = _dot(b_ref[...], wb_ref[...])
    o_ref[...] = (g0_ref[...] * ya + g1_ref[...] * yb).astype(o_ref.dtype)


def _proj_ln_kernel(x_ref, a_ref, w_ref, g_ref, b_ref, o_ref, obf_ref):
    y = DEEPNORM_ALPHA * x_ref[...] + _dot(a_ref[...], w_ref[...])
    out = _layer_norm(y, g_ref[...], b_ref[...])
    o_ref[...] = out
    obf_ref[...] = out.astype(BF16)


def _ffn_up_kernel(x_ref, xp_ref, wg_ref, wv_ref, cwg_ref, cwv_ref, cbg_ref, cbv_ref, o_ref, *, tm, seq):
    i = pl.program_id(1)
    first = (i * tm) % seq == 0
    x = x_ref[...]
    xp = xp_ref[...]
    row = lax.broadcasted_iota(jnp.int32, (tm, 1), 0)

    def conv(w_ref, cw_ref, cb_ref):
        h = _dot(x, w_ref[...])
        hp = jnp.where(first, 0.0, _dot(xp, w_ref[...]))
        hm1 = jnp.where(row == 0, hp[7:8, :], pltpu.roll(h, 1, 0))
        hm2 = jnp.where(row == 0, hp[6:7, :], jnp.where(row == 1, hp[7:8, :], pltpu.roll(h, 2, 0)))
        return cw_ref[0:1, :] * hm2 + cw_ref[1:2, :] * hm1 + cw_ref[2:3, :] * h + cb_ref[...]

    hg = conv(wg_ref, cwg_ref, cbg_ref)
    hv = conv(wv_ref, cwv_ref, cbv_ref)
    o_ref[...] = (hg * jax.nn.sigmoid(hg) * hv).astype(o_ref.dtype)


def _ple_kernel(x_ref, p_ref, wpg_ref, bpg_ref, wpe_ref, o_ref):
    gate = jax.nn.sigmoid(_dot(x_ref[...], wpg_ref[...]) + bpg_ref[...])
    o_ref[...] = gate * _dot(p_ref[...], wpe_ref[...])


def _down_ln_kernel(x_ref, a_ref, w_ref, ple_ref, g_ref, b_ref, o_ref, acc_ref):
    k = pl.program_id(1)

    @pl.when(k == 0)
    def _():
        acc_ref[...] = DEEPNORM_ALPHA * x_ref[...] + ple_ref[...]

    acc_ref[...] += _dot(a_ref[...], w_ref[...])

    @pl.when(k == pl.num_programs(1) - 1)
    def _():
        o_ref[...] = _layer_norm(acc_ref[...], g_ref[...], b_ref[...])


def _split_bf16(x, parts):
    out = []
    rem = x
    for _ in range(parts):
        piece = rem.astype(BF16)
        out.append(piece)
        rem = rem - piece.astype(F32)
    return out


def kernel(x, p, w_in, q_a_norm, kv_a_norm, w_uq, w_uk, w_uv, w_branch_a, w_branch_b, w_o,
           ln1_g, ln1_b, w_up, conv_w, conv_b, w_down, w_pg, b_pg, w_pe, ln2_g, ln2_b):
    bsz, seq, d = x.shape
    m = bsz * seq
    topk = min(INDEX_TOPK_MAX, seq // 4)
    row = lambda a: a.reshape(1, -1)

    wi = w_in[0]
    o_qa, o_ka, o_va = 0, 2048, 2176
    o_qi, o_ki, o_wi, o_cq, o_ckv, o_kpe, o_gate = 2304, 3328, 3392, 3408, 3920, 4432, 4496
    w_qa = wi[:, o_qa:o_ka].astype(BF16)
    w_kv = wi[:, o_ka:o_qi].astype(BF16)
    w_qi = wi[:, o_qi:o_ki].astype(BF16)
    w_misc = jnp.pad(wi[:, o_ki:o_cq], ((0, 0), (0, LANES - (o_cq - o_ki)))).astype(BF16)
    w_lat = jnp.pad(wi[:, o_cq:o_gate], ((0, 0), (0, LANES - QK_ROPE))).astype(BF16)
    w_gate = wi[:, o_gate:].astype(BF16)

    wq = jnp.pad(w_uq[0], ((0, 0), (0, 0), (0, QK_PAD - QK_NOPE - QK_ROPE)))
    wq = wq.reshape(Q_LORA, B_HEADS * QK_PAD).astype(BF16)
    wk = w_uk[0].reshape(KV_LORA, B_HEADS * QK_NOPE).astype(BF16)
    wv = w_uv[0].reshape(KV_LORA, B_HEADS * V_HEAD).astype(BF16)

    pos = jnp.arange(seq, dtype=F32)
    inv_freq = jnp.power(ROPE_THETA, -jnp.arange(0, QK_ROPE, 2, dtype=F32) / QK_ROPE)
    ang = pos[:, None] * inv_freq[None, :]
    cos, sin = jnp.cos(ang), jnp.sin(ang)
    zpad = jnp.zeros((seq, LANES - QK_ROPE), F32)
    cos2 = jnp.concatenate([cos, cos, zpad], axis=1)
    sin2 = jnp.concatenate([-sin, sin, zpad], axis=1)

    slopes = jnp.exp2(-8.0 * jnp.arange(1, A_HEADS + 1, dtype=F32) / A_HEADS)
    sl = _split_bf16(slopes, 3)
    aug = jnp.stack([s_ * 64.0 for s_ in sl] + sl, axis=1).astype(BF16)
    aug = jnp.pad(aug, ((0, 0), (0, LANES - 6)))
    q_blk = 128
    aug_rows = jnp.repeat(aug, q_blk, axis=0)
    s_int = jnp.arange(seq, dtype=jnp.int32)
    s_hi = (s_int // 64).astype(BF16)
    s_lo = (s_int % 64).astype(BF16)
    kaug = jnp.stack([s_hi, s_hi, s_hi, s_lo, s_lo, s_lo], axis=1)
    kaug = jnp.pad(kaug, ((0, 0), (0, LANES - 6)))

    x2 = x.reshape(m, d)
    xb = x2.astype(BF16)

    qa = _matmul(xb, w_qa, BF16, scale=A_HEAD_DIM ** -0.5)
    kv = _matmul(xb, w_kv, BF16)
    qidx = _matmul(xb, w_qi, BF16)
    misc = _matmul(xb, w_misc, F32)
    lat = _matmul(xb, w_lat, F32, tn=1152)
    gates = _matmul(xb, w_gate, F32, act="sigmoid")

    tm = 256
    nst = seq // tm
    q_mla = pl.pallas_call(
        functools.partial(_mla_q_kernel, scale=(QK_NOPE + QK_ROPE) ** -0.5),
        grid=(m // tm,),
        in_specs=[pl.BlockSpec((tm, Q_LORA), lambda i: (i, 0)),
                  pl.BlockSpec((1, Q_LORA), lambda i: (0, 0)),
                  pl.BlockSpec((Q_LORA, B_HEADS * QK_PAD), lambda i: (0, 0)),
                  pl.BlockSpec((tm, LANES), lambda i: (i % nst, 0)),
                  pl.BlockSpec((tm, LANES), lambda i: (i % nst, 0))],
        out_specs=pl.BlockSpec((tm, B_HEADS * QK_PAD), lambda i: (i, 0)),
        out_shape=jax.ShapeDtypeStruct((m, B_HEADS * QK_PAD), BF16),
        compiler_params=_cparams(("parallel",)),
        name="mla_q_proj",
    )(lat, row(q_a_norm[0]), wq, cos2, sin2)

    k_mla, v_mla = pl.pallas_call(
        _mla_kv_kernel,
        grid=(m // tm,),
        in_specs=[pl.BlockSpec((tm, KV_LORA), lambda i: (i, 1)),
                  pl.BlockSpec((tm, LANES), lambda i: (i, (Q_LORA + KV_LORA) // LANES)),
                  pl.BlockSpec((1, KV_LORA), lambda i: (0, 0)),
                  pl.BlockSpec((KV_LORA, B_HEADS * QK_NOPE), lambda i: (0, 0)),
                  pl.BlockSpec((KV_LORA, B_HEADS * V_HEAD), lambda i: (0, 0)),
                  pl.BlockSpec((tm, LANES), lambda i: (i % nst, 0)),
                  pl.BlockSpec((tm, LANES), lambda i: (i % nst, 0))],
        out_specs=[pl.BlockSpec((tm, B_HEADS * QK_PAD), lambda i: (i, 0)),
                   pl.BlockSpec((tm, B_HEADS * V_HEAD), lambda i: (i, 0))],
        out_shape=[jax.ShapeDtypeStruct((m, B_HEADS * QK_PAD), BF16),
                   jax.ShapeDtypeStruct((m, B_HEADS * V_HEAD), BF16)],
        compiler_params=_cparams(("parallel",)),
        name="mla_kv_proj",
    )(lat, lat, row(kv_a_norm[0]), wk, wv, cos2, sin2)

    tq = 512
    attn_b = pl.pallas_call(
        functools.partial(_mla_attn_kernel, tq=tq),
        grid=(bsz, B_HEADS, seq // tq),
        in_specs=[pl.BlockSpec((1, tq, QK_PAD), lambda b, h, i: (b, i, h)),
                  pl.BlockSpec((1, seq, QK_PAD), lambda b, h, i: (b, 0, h)),
                  pl.BlockSpec((1, seq, V_HEAD), lambda b, h, i: (b, 0, h))],
        out_specs=pl.BlockSpec((1, tq, V_HEAD), lambda b, h, i: (b, i, h)),
        out_shape=jax.ShapeDtypeStruct((bsz, seq, B_HEADS * V_HEAD), BF16),
        scratch_shapes=[pltpu.VMEM((tq, 1), F32), pltpu.VMEM((tq, 1), F32), pltpu.VMEM((tq, V_HEAD), F32)],
        compiler_params=_cparams(("parallel", "parallel", "arbitrary")),
        name="mla_attn",
    )(q_mla.reshape(bsz, seq, -1), k_mla.reshape(bsz, seq, -1), v_mla.reshape(bsz, seq, -1))

    k_a = kv[:, :A_HEAD_DIM].reshape(bsz, seq, A_HEAD_DIM)
    v_a = kv[:, A_HEAD_DIM:].reshape(bsz, seq, A_HEAD_DIM)
    kext = jnp.concatenate([k_a, jnp.broadcast_to(kaug[None], (bsz, seq, LANES))], axis=-1)
    k_idx = misc[:, :IDX_DIM].astype(BF16).reshape(bsz, seq, IDX_DIM)
    zk = jnp.zeros_like(k_idx)
    k_even = jnp.concatenate([k_idx, zk], axis=-1)
    k_odd = jnp.concatenate([zk, k_idx], axis=-1)
    w_idx = misc[:, IDX_DIM:IDX_DIM + IDX_HEADS].reshape(bsz, seq, IDX_HEADS)
    chunk, group = 512, 4
    attn_a = pl.pallas_call(
        functools.partial(_dsa_kernel, q_blk=q_blk, topk=topk, chunk=chunk, group=group),
        grid=(bsz, seq // q_blk),
        in_specs=[pl.BlockSpec((1, q_blk, A_HEADS * A_HEAD_DIM), lambda b, i: (b, i, 0)),
                  pl.BlockSpec((1, q_blk, IDX_HEADS * IDX_DIM), lambda b, i: (b, i, 0)),
                  pl.BlockSpec((1, q_blk, IDX_HEADS), lambda b, i: (b, i, 0)),
                  pl.BlockSpec((1, seq, 2 * A_HEAD_DIM), lambda b, i: (b, 0, 0)),
                  pl.BlockSpec((1, seq, A_HEAD_DIM), lambda b, i: (b, 0, 0)),
                  pl.BlockSpec((1, seq, LANES), lambda b, i: (b, 0, 0)),
                  pl.BlockSpec((1, seq, LANES), lambda b, i: (b, 0, 0)),
                  pl.BlockSpec((A_HEADS * q_blk, LANES), lambda b, i: (0, 0))],
        out_specs=pl.BlockSpec((1, q_blk, A_HEADS * A_HEAD_DIM), lambda b, i: (b, i, 0)),
        out_shape=jax.ShapeDtypeStruct((bsz, seq, A_HEADS * A_HEAD_DIM), BF16),
        scratch_shapes=[pltpu.VMEM((q_blk, seq), jnp.int32),
                        pltpu.VMEM((A_HEADS * q_blk, 2 * A_HEAD_DIM), BF16),
                        pltpu.VMEM((IDX_HEADS // 2 * q_blk, LANES), BF16),
                        pltpu.VMEM((IDX_HEADS, q_blk, LANES), F32),
                        pltpu.VMEM((A_HEADS * q_blk, 1), F32),
                        pltpu.VMEM((A_HEADS * q_blk, 1), F32),
                        pltpu.VMEM((A_HEADS * q_blk, A_HEAD_DIM), F32)],
        compiler_params=_cparams(("parallel", "arbitrary")),
        name="dsa_attn",
    )(qa.reshape(bsz, seq, -1), qidx.reshape(bsz, seq, -1), w_idx, kext, v_a, k_even, k_odd, aug_rows)

    tmm, tnm = 1024, 512
    merged = pl.pallas_call(
        _merge_kernel,
        grid=(d // tnm, m // tmm),
        in_specs=[pl.BlockSpec((tmm, d), lambda j, i: (i, 0)),
                  pl.BlockSpec((tmm, d), lambda j, i: (i, 0)),
                  pl.BlockSpec((d, tnm), lambda j, i: (0, j)),
                  pl.BlockSpec((d, tnm), lambda j, i: (0, j)),
                  pl.BlockSpec((tmm, tnm), lambda j, i: (i, j)),
                  pl.BlockSpec((tmm, tnm), lambda j, i: (i, j + d // tnm))],
        out_specs=pl.BlockSpec((tmm, tnm), lambda j, i: (i, j)),
        out_shape=jax.ShapeDtypeStruct((m, d), BF16),
        compiler_params=_cparams(("parallel", "parallel")),
        name="branch_merge",
    )(attn_a.reshape(m, d), attn_b.reshape(m, d), w_branch_a[0].astype(BF16), w_branch_b[0].astype(BF16),
      gates, gates)

    tml = 256
    x1, x1b = pl.pallas_call(
        _proj_ln_kernel,
        grid=(m // tml,),
        in_specs=[pl.BlockSpec((tml, d), lambda i: (i, 0)),
                  pl.BlockSpec((tml, d), lambda i: (i, 0)),
                  pl.BlockSpec((d, d), lambda i: (0, 0)),
                  pl.BlockSpec((1, d), lambda i: (0, 0)),
                  pl.BlockSpec((1, d), lambda i: (0, 0))],
        out_specs=[pl.BlockSpec((tml, d), lambda i: (i, 0)), pl.BlockSpec((tml, d), lambda i: (i, 0))],
        out_shape=[jax.ShapeDtypeStruct((m, d), F32), jax.ShapeDtypeStruct((m, d), BF16)],
        compiler_params=_cparams(("parallel",)),
        name="oproj_ln1",
    )(x2, merged, w_o[0].astype(BF16), row(ln1_g[0]), row(ln1_b[0]))

    tmf, tnf = 1024, 512
    nft = D_FF // tnf
    w_up_b = w_up[0].astype(BF16)
    cw, cb = conv_w[0], row(conv_b[0])
    act = pl.pallas_call(
        functools.partial(_ffn_up_kernel, tm=tmf, seq=seq),
        grid=(nft, m // tmf),
        in_specs=[pl.BlockSpec((tmf, d), lambda j, i: (i, 0)),
                  pl.BlockSpec((8, d), lambda j, i: (jnp.maximum(i * (tmf // 8) - 1, 0), 0)),
                  pl.BlockSpec((d, tnf), lambda j, i: (0, j)),
                  pl.BlockSpec((d, tnf), lambda j, i: (0, j + nft)),
                  pl.BlockSpec((CONV_W, tnf), lambda j, i: (0, j)),
                  pl.BlockSpec((CONV_W, tnf), lambda j, i: (0, j + nft)),
                  pl.BlockSpec((1, tnf), lambda j, i: (0, j)),
                  pl.BlockSpec((1, tnf), lambda j, i: (0, j + nft))],
        out_specs=pl.BlockSpec((tmf, tnf), lambda j, i: (i, j)),
        out_shape=jax.ShapeDtypeStruct((m, D_FF), BF16),
        compiler_params=_cparams(("parallel", "parallel")),
        name="ffn_up_conv",
    )(x1b, x1b, w_up_b, w_up_b, cw, cw, cb, cb)

    tnp = 1024
    ple = pl.pallas_call(
        _ple_kernel,
        grid=(d // tnp, m // tmm),
        in_specs=[pl.BlockSpec((tmm, d), lambda j, i: (i, 0)),
                  pl.BlockSpec((tmm, PLE_DIM), lambda j, i: (i, 0)),
                  pl.BlockSpec((d, tnp), lambda j, i: (0, j)),
                  pl.BlockSpec((1, tnp), lambda j, i: (0, j)),
                  pl.BlockSpec((PLE_DIM, tnp), lambda j, i: (0, j))],
        out_specs=pl.BlockSpec((tmm, tnp), lambda j, i: (i, j)),
        out_shape=jax.ShapeDtypeStruct((m, d), F32),
        compiler_params=_cparams(("parallel", "parallel")),
        name="ple_gate",
    )(x1b, p[0].reshape(m, PLE_DIM).astype(BF16), w_pg[0].astype(BF16), row(b_pg[0]), w_pe[0].astype(BF16))

    tmd, tkd = 512, 512
    out = pl.pallas_call(
        _down_ln_kernel,
        grid=(m // tmd, D_FF // tkd),
        in_specs=[pl.BlockSpec((tmd, d), lambda i, k: (i, 0)),
                  pl.BlockSpec((tmd, tkd), lambda i, k: (i, k)),
                  pl.BlockSpec((tkd, d), lambda i, k: (k, 0)),
                  pl.BlockSpec((tmd, d), lambda i, k: (i, 0)),
                  pl.BlockSpec((1, d), lambda i, k: (0, 0)),
                  pl.BlockSpec((1, d), lambda i, k: (0, 0))],
        out_specs=pl.BlockSpec((tmd, d), lambda i, k: (i, 0)),
        out_shape=jax.ShapeDtypeStruct((m, d), F32),
        scratch_shapes=[pltpu.VMEM((tmd, d), F32)],
        compiler_params=_cparams(("parallel", "arbitrary")),
        name="ffn_down_ln2",
    )(x1, act, w_down[0].astype(BF16), ple, row(ln2_g[0]), row(ln2_b[0]))

    return out.reshape(bsz, seq, d)
```

```python
import functools

import jax
import jax.numpy as jnp
import numpy as np
from jax import lax
from jax.experimental import pallas as pl
from jax.experimental.pallas import tpu as pltpu

F32 = jnp.float32
BF16 = jnp.bfloat16

D_MODEL = 2048
A_HEADS = 16
A_HEAD_DIM = 128
IDX_HEADS = 16
IDX_DIM = 64
INDEX_TOPK_MAX = 256
B_HEADS = 16
Q_LORA = 512
KV_LORA = 512
QK_NOPE = 128
QK_ROPE = 64
V_HEAD = 128
ROPE_THETA = 10000.0
D_FF = 5632
CONV_W = 3
PLE_DIM = 256
LN_EPS = 1e-5
RMS_EPS = 1e-6
NEG = -1e30
DEEPNORM_ALPHA = 2.0 ** 0.25
LOG2E = 1.4426950408889634

LANES = 128
QK_PAD = 256
VMEM_LIMIT = 56 * 1024 * 1024
INT_MIN = -2 ** 31
INT_MAX = 2 ** 31 - 1


def _cparams(sem):
    return pltpu.CompilerParams(dimension_semantics=sem, vmem_limit_bytes=VMEM_LIMIT)


def _dot(a, b):
    return jnp.dot(a, b, preferred_element_type=F32)


def _dot_nt(a, b):
    return lax.dot_general(a, b, (((1,), (1,)), ((), ())), preferred_element_type=F32)


def _layer_norm(y, g, b):
    mu = jnp.mean(y, axis=-1, keepdims=True)
    d = y - mu
    var = jnp.mean(d * d, axis=-1, keepdims=True)
    return d * lax.rsqrt(var + LN_EPS) * g + b


def _rms_norm(c, g):
    return c * lax.rsqrt(jnp.mean(c * c, axis=-1, keepdims=True) + RMS_EPS) * g


def _rope_lanes(pe, cos2, sin2):
    lane = lax.broadcasted_iota(jnp.int32, pe.shape, 1)
    swapped = jnp.where(lane < QK_ROPE // 2, pltpu.roll(pe, LANES - QK_ROPE // 2, 1),
                        pltpu.roll(pe, QK_ROPE // 2, 1))
    return pe * cos2 + swapped * sin2


def _mm_kernel(x_ref, w_ref, o_ref, *, scale, act):
    y = _dot(x_ref[...], w_ref[...])
    if scale != 1.0:
        y = y * scale
    if act == "sigmoid":
        y = jax.nn.sigmoid(y)
    o_ref[...] = y.astype(o_ref.dtype)


def _matmul(x, w, out_dtype, *, scale=1.0, act=None, tm=1024, tn=1024):
    m, k = x.shape
    n = w.shape[1]
    tn = min(tn, n)
    assert m % tm == 0 and n % tn == 0
    return pl.pallas_call(
        functools.partial(_mm_kernel, scale=scale, act=act),
        grid=(n // tn, m // tm),
        in_specs=[pl.BlockSpec((tm, k), lambda j, i: (i, 0)),
                  pl.BlockSpec((k, tn), lambda j, i: (0, j))],
        out_specs=pl.BlockSpec((tm, tn), lambda j, i: (i, j)),
        out_shape=jax.ShapeDtypeStruct((m, n), out_dtype),
        compiler_params=_cparams(("parallel", "parallel")),
        name="proj_mm",
    )(x, w)


def _mla_q_kernel(c_ref, g_ref, w_ref, cos_ref, sin_ref, o_ref, *, scale):
    y = _rms_norm(c_ref[...], g_ref[...]).astype(BF16)
    q = _dot(y, w_ref[...])
    cos2 = cos_ref[...]
    sin2 = sin_ref[...]
    for h in range(B_HEADS):
        base = h * QK_PAD
        o_ref[:, base:base + QK_NOPE] = (q[:, base:base + QK_NOPE] * scale).astype(BF16)
        pe = q[:, base + QK_NOPE:base + QK_PAD]
        o_ref[:, base + QK_NOPE:base + QK_PAD] = (_rope_lanes(pe, cos2, sin2) * scale).astype(BF16)


def _mla_kv_kernel(c_ref, kpe_ref, g_ref, wk_ref, wv_ref, cos_ref, sin_ref, k_ref, v_ref):
    y = _rms_norm(c_ref[...], g_ref[...]).astype(BF16)
    kn = _dot(y, wk_ref[...])
    v_ref[...] = _dot(y, wv_ref[...]).astype(BF16)
    kpe = _rope_lanes(kpe_ref[...], cos_ref[...], sin_ref[...]).astype(BF16)
    for h in range(B_HEADS):
        base = h * QK_PAD
        k_ref[:, base:base + QK_NOPE] = kn[:, h * QK_NOPE:(h + 1) * QK_NOPE].astype(BF16)
        k_ref[:, base + QK_NOPE:base + QK_PAD] = kpe


def _flash_update(s, v_chunk, m_ref, l_ref, acc_ref, rows):
    nblk = s.shape[1] // LANES
    m_prev = m_ref[rows, :]
    m_new = jnp.maximum(m_prev, jnp.max(s, axis=-1, keepdims=True))
    alpha = jnp.exp2(m_prev - m_new)
    ps = [jnp.exp2(s[:, j * LANES:(j + 1) * LANES] - m_new) for j in range(nblk)]
    l_ref[rows, :] = alpha * l_ref[rows, :] + functools.reduce(lambda a, b: a + b, ps)
    p = jnp.concatenate([x.astype(BF16) for x in ps], axis=1)
    acc_ref[rows, :] = alpha * acc_ref[rows, :] + _dot(p, v_chunk)
    m_ref[rows, :] = m_new


def _flash_finish(l_ref, acc_ref, rows):
    return acc_ref[rows, :] / jnp.sum(l_ref[rows, :], axis=-1, keepdims=True)


def _mla_attn_kernel(q_ref, k_ref, v_ref, o_ref, m_ref, l_ref, acc_ref, *, tq, hp):
    qi = pl.program_id(2)
    m_ref[...] = jnp.full(m_ref.shape, NEG, F32)
    l_ref[...] = jnp.zeros(l_ref.shape, F32)
    acc_ref[...] = jnp.zeros(acc_ref.shape, F32)

    def step(c, masked):
        off = pl.multiple_of(c * tq, tq)
        for h in range(hp):
            q = q_ref[0, :, h * QK_PAD:(h + 1) * QK_PAD]
            s = _dot_nt(q, k_ref[0, pl.ds(off, tq), h * QK_PAD:(h + 1) * QK_PAD])
            if masked:
                row = lax.broadcasted_iota(jnp.int32, s.shape, 0)
                col = lax.broadcasted_iota(jnp.int32, s.shape, 1)
                s = jnp.where(col <= row, s, NEG)
            _flash_update(s, v_ref[0, pl.ds(off, tq), h * V_HEAD:(h + 1) * V_HEAD],
                          m_ref, l_ref, acc_ref, slice(h * tq, (h + 1) * tq))

    def body(c, carry):
        step(c, False)
        return carry

    lax.fori_loop(0, qi, body, 0)
    step(qi, True)
    for h in range(hp):
        o_ref[0, :, h * V_HEAD:(h + 1) * V_HEAD] = _flash_finish(
            l_ref, acc_ref, slice(h * tq, (h + 1) * tq)).astype(o_ref.dtype)


def _monotone_key(x):
    bits = lax.bitcast_convert_type(x, jnp.int32)
    return bits ^ ((bits >> 31) & INT_MAX)


def _dsa_kernel(qa_ref, qi_ref, w_ref, kext_ref, v_ref, ke_ref, ko_ref, aug_ref, o_ref,
                key_ref, lhs_ref, qs_ref, wb_ref, m_ref, l_ref, acc_ref,
                *, q_blk, topk, chunk, group):
    qb = pl.program_id(1)
    n_chunks = (qb * q_blk + q_blk + chunk - 1) // chunk
    rep = chunk // LANES
    n_pairs = IDX_HEADS // 2

    for h in range(A_HEADS):
        lhs_ref[h * q_blk:(h + 1) * q_blk, 0:A_HEAD_DIM] = qa_ref[0, :, h * A_HEAD_DIM:(h + 1) * A_HEAD_DIM]
    lhs_ref[:, A_HEAD_DIM:2 * A_HEAD_DIM] = aug_ref[...]
    for j in range(n_pairs):
        qs_ref[j * q_blk:(j + 1) * q_blk, :] = qi_ref[0, :, j * LANES:(j + 1) * LANES]
    for h in range(IDX_HEADS):
        wb_ref[h] = jnp.broadcast_to(w_ref[0, :, h:h + 1], (q_blk, LANES))

    t_col = qb * q_blk + lax.broadcasted_iota(jnp.int32, (q_blk, 1), 0)

    def score_chunk(c, carry):
        off = pl.multiple_of(c * chunk, chunk)
        qs = qs_ref[...]
        se = _dot_nt(qs, ke_ref[0, pl.ds(off, chunk), :])
        so = _dot_nt(qs, ko_ref[0, pl.ds(off, chunk), :])
        acc = jnp.zeros((q_blk, chunk), F32)
        for j in range(n_pairs):
            we = jnp.concatenate([wb_ref[2 * j]] * rep, axis=1)
            wo = jnp.concatenate([wb_ref[2 * j + 1]] * rep, axis=1)
            acc = acc + we * jnp.maximum(se[j * q_blk:(j + 1) * q_blk], 0.0)
            acc = acc + wo * jnp.maximum(so[j * q_blk:(j + 1) * q_blk], 0.0)
        s_pos = off + lax.broadcasted_iota(jnp.int32, (q_blk, chunk), 1)
        key_ref[:, pl.ds(off, chunk)] = jnp.where(s_pos <= t_col, _monotone_key(acc), INT_MIN)
        return carry

    lax.fori_loop(0, n_chunks, score_chunk, 0)

    kf = float(topk)

    def count_ge(th):
        thb = jnp.broadcast_to(th, (q_blk, LANES))

        def body(c, cnt):
            off = pl.multiple_of(c * chunk, chunk)
            blk = key_ref[:, pl.ds(off, chunk)]
            for j in range(rep):
                cnt = cnt + jnp.where(blk[:, j * LANES:(j + 1) * LANES] >= thb, 1.0, 0.0)
            return cnt

        cnt = lax.fori_loop(0, n_chunks, body, jnp.zeros((q_blk, LANES), F32))
        return jnp.sum(cnt, axis=-1, keepdims=True)

    n_causal = (t_col + 1).astype(F32)
    few = n_causal <= kf
    lo0 = jnp.full((q_blk, 1), INT_MIN + 1, jnp.int32)
    hi0 = jnp.where(few, INT_MIN + 2, INT_MAX).astype(jnp.int32)
    cnt_lo0 = n_causal
    cnt_hi0 = jnp.zeros((q_blk, 1), F32)

    def active(lo, hi, cnt_lo):
        return jnp.logical_and(lo + 1 < hi, cnt_lo != kf)

    def bis_cond(st):
        it, lo, hi, cnt_lo, cnt_hi = st
        n_act = jnp.max(jnp.where(active(lo, hi, cnt_lo), 1, 0))
        return jnp.logical_and(it < 34, n_act > 0)

    def bis_body(st):
        it, lo, hi, cnt_lo, cnt_hi = st
        act = active(lo, hi, cnt_lo)
        mid = (lo & hi) + ((lo ^ hi) >> 1)
        c = count_ge(mid)
        ge = jnp.logical_and(act, c >= kf)
        lt = jnp.logical_and(act, c < kf)
        return (it + 1, jnp.where(ge, mid, lo), jnp.where(lt, mid, hi),
                jnp.where(ge, c, cnt_lo), jnp.where(lt, c, cnt_hi))

    _, v_key, _, cnt_lo, cnt_hi = lax.while_loop(
        bis_cond, bis_body, (jnp.int32(0), lo0, hi0, cnt_lo0, cnt_hi0))

    tied = jnp.logical_and(cnt_lo > kf, jnp.logical_not(few))
    need = kf - cnt_hi
    n_tied = jnp.max(jnp.where(tied, 1, 0))

    def count_tie_le(jj):
        vb = jnp.broadcast_to(v_key, (q_blk, LANES))
        jb = jnp.broadcast_to(jj, (q_blk, LANES))
        lane = lax.broadcasted_iota(jnp.int32, (q_blk, LANES), 1)

        def body(c, cnt):
            off = pl.multiple_of(c * chunk, chunk)
            blk = key_ref[:, pl.ds(off, chunk)]
            for j in range(rep):
                in_range = jnp.where(off + j * LANES + lane <= jb, 1.0, 0.0)
                cnt = cnt + jnp.where(blk[:, j * LANES:(j + 1) * LANES] == vb, in_range, 0.0)
            return cnt

        cnt = lax.fori_loop(0, n_chunks, body, jnp.zeros((q_blk, LANES), F32))
        return jnp.sum(cnt, axis=-1, keepdims=True)

    def tie_search(_):
        def body(_, st):
            lo, hi = st
            mid = (lo + hi) >> 1
            ok = count_tie_le(mid) >= need
            upd = lo + 1 < hi
            return (jnp.where(jnp.logical_and(upd, jnp.logical_not(ok)), mid, lo),
                    jnp.where(jnp.logical_and(upd, ok), mid, hi))

        lo = jnp.full((q_blk, 1), -1, jnp.int32)
        hi = jnp.broadcast_to(t_col, (q_blk, 1))
        _, hi = lax.fori_loop(0, 14, body, (lo, hi))
        return jnp.where(tied, hi, INT_MAX)

    j_max = lax.cond(n_tied > 0, tie_search, lambda _: jnp.full((q_blk, 1), INT_MAX, jnp.int32), 0)

    m_ref[...] = jnp.full(m_ref.shape, NEG, F32)
    l_ref[...] = jnp.zeros(l_ref.shape, F32)
    acc_ref[...] = jnp.zeros(acc_ref.shape, F32)
    rows = group * q_blk
    n_groups = A_HEADS // group

    def attn_chunk(c, carry):
        off = pl.multiple_of(c * chunk, chunk)
        keys = key_ref[:, pl.ds(off, chunk)]
        s_pos = off + lax.broadcasted_iota(jnp.int32, (q_blk, chunk), 1)
        sel = jnp.where(keys > v_key, 1, jnp.where(keys == v_key, jnp.where(s_pos <= j_max, 1, 0), 0))
        bias = jnp.where(sel > 0, 0.0, NEG)
        kc = kext_ref[0, pl.ds(off, chunk), :]
        vc = v_ref[0, pl.ds(off, chunk), :]
        for g in range(n_groups):
            r0 = g * rows
            s = _dot_nt(lhs_ref[r0:r0 + rows, :], kc)
            s = (s.reshape(group, q_blk, chunk) + bias[None]).reshape(rows, chunk)
            _flash_update(s, vc, m_ref, l_ref, acc_ref, slice(r0, r0 + rows))
        return carry

    lax.fori_loop(0, n_chunks, attn_chunk, 0)
    for h in range(A_HEADS):
        r0 = h * q_blk
        o_ref[0, :, h * A_HEAD_DIM:(h + 1) * A_HEAD_DIM] = _flash_finish(
            l_ref, acc_ref, slice(r0, r0 + q_blk)).astype(o_ref.dtype)


def _merge_kernel(a_ref, b_ref, wa_ref, wb_ref, g0_ref, g1_ref, o_ref):
    ya = _dot(a_ref[...], wa_ref[...])
    yb = _dot(b_ref[...], wb_ref[...])
    o_ref[...] = (g0_ref[...] * ya + g1_ref[...] * yb).astype(o_ref.dtype)


def _proj_ln_kernel(x_ref, a_ref, w_ref, g_ref, b_ref, o_ref, obf_ref):
    y = DEEPNORM_ALPHA * x_ref[...] + _dot(a_ref[...], w_ref[...])
    out = _layer_norm(y, g_ref[...], b_ref[...])
    o_ref[...] = out
    obf_ref[...] = out.astype(BF16)


def _ffn_up_kernel(x_ref, xp_ref, wg_ref, wv_ref, cwg_ref, cwv_ref, cbg_ref, cbv_ref, o_ref, *, tm, seq):
    i = pl.program_id(1)
    first = (i * tm) % seq == 0
    x = x_ref[...]
    xp = xp_ref[...]
    row = lax.broadcasted_iota(jnp.int32, (tm, 1), 0)

    def conv(w_ref, cw_ref, cb_ref):
        h = _dot(x, w_ref[...])
        hp = jnp.where(first, 0.0, _dot(xp, w_ref[...]))
        hm1 = jnp.where(row == 0, hp[7:8, :], pltpu.roll(h, 1, 0))
        hm2 = jnp.where(row == 0, hp[6:7, :], jnp.where(row == 1, hp[7:8, :], pltpu.roll(h, 2, 0)))
        return cw_ref[0:1, :] * hm2 + cw_ref[1:2, :] * hm1 + cw_ref[2:3, :] * h + cb_ref[...]

    hg = conv(wg_ref, cwg_ref, cbg_ref)
    hv = conv(wv_ref, cwv_ref, cbv_ref)
    o_ref[...] = (hg * jax.nn.sigmoid(hg) * hv).astype(o_ref.dtype)


def _ple_kernel(x_ref, p_ref, wpg_ref, bpg_ref, wpe_ref, o_ref):
    gate = jax.nn.sigmoid(_dot(x_ref[...], wpg_ref[...]) + bpg_ref[...])
    o_ref[...] = gate * _dot(p_ref[...], wpe_ref[...])


def _down_ln_kernel(x_ref, a_ref, w_ref, ple_ref, g_ref, b_ref, o_ref, acc_ref):
    k = pl.program_id(1)

    @pl.when(k == 0)
    def _():
        acc_ref[...] = DEEPNORM_ALPHA * x_ref[...] + ple_ref[...]

    acc_ref[...] += _dot(a_ref[...], w_ref[...])

    @pl.when(k == pl.num_programs(1) - 1)
    def _():
        o_ref[...] = _layer_norm(acc_ref[...], g_ref[...], b_ref[...])


def _split_bf16(x, parts):
    out = []
    rem = x
    for _ in range(parts):
        piece = rem.astype(BF16)
        out.append(piece)
        rem = rem - piece.astype(F32)
    return out


def kernel(x, p, w_in, q_a_norm, kv_a_norm, w_uq, w_uk, w_uv, w_branch_a, w_branch_b, w_o,
           ln1_g, ln1_b, w_up, conv_w, conv_b, w_down, w_pg, b_pg, w_pe, ln2_g, ln2_b):
    bsz, seq, d = x.shape
    m = bsz * seq
    topk = min(INDEX_TOPK_MAX, seq // 4)
    row = lambda a: a.reshape(1, -1)

    wi = w_in[0]
    o_qa, o_ka, o_va = 0, 2048, 2176
    o_qi, o_ki, o_wi, o_cq, o_ckv, o_kpe, o_gate = 2304, 3328, 3392, 3408, 3920, 4432, 4496
    w_qa = wi[:, o_qa:o_ka].astype(BF16)
    w_kv = wi[:, o_ka:o_qi].astype(BF16)
    w_qi = wi[:, o_qi:o_ki].astype(BF16)
    w_misc = jnp.pad(wi[:, o_ki:o_cq], ((0, 0), (0, LANES - (o_cq - o_ki)))).astype(BF16)
    w_lat = jnp.pad(wi[:, o_cq:o_gate], ((0, 0), (0, LANES - QK_ROPE))).astype(BF16)
    w_gate = wi[:, o_gate:].astype(BF16)

    wq = jnp.pad(w_uq[0], ((0, 0), (0, 0), (0, QK_PAD - QK_NOPE - QK_ROPE)))
    wq = wq.reshape(Q_LORA, B_HEADS * QK_PAD).astype(BF16)
    wk = w_uk[0].reshape(KV_LORA, B_HEADS * QK_NOPE).astype(BF16)
    wv = w_uv[0].reshape(KV_LORA, B_HEADS * V_HEAD).astype(BF16)

    pos = jnp.arange(seq, dtype=F32)
    inv_freq = jnp.power(ROPE_THETA, -jnp.arange(0, QK_ROPE, 2, dtype=F32) / QK_ROPE)
    ang = pos[:, None] * inv_freq[None, :]
    cos, sin = jnp.cos(ang), jnp.sin(ang)
    zpad = jnp.zeros((seq, LANES - QK_ROPE), F32)
    cos2 = jnp.concatenate([cos, cos, zpad], axis=1)
    sin2 = jnp.concatenate([-sin, sin, zpad], axis=1)

    slopes = LOG2E * jnp.exp2(-8.0 * jnp.arange(1, A_HEADS + 1, dtype=F32) / A_HEADS)
    sl = _split_bf16(slopes, 3)
    aug = jnp.stack([s_ * 64.0 for s_ in sl] + sl, axis=1).astype(BF16)
    aug = jnp.pad(aug, ((0, 0), (0, LANES - 6)))
    q_blk = 128
    aug_rows = jnp.repeat(aug, q_blk, axis=0)
    s_int = jnp.arange(seq, dtype=jnp.int32)
    s_hi = (s_int // 64).astype(BF16)
    s_lo = (s_int % 64).astype(BF16)
    kaug = jnp.stack([s_hi, s_hi, s_hi, s_lo, s_lo, s_lo], axis=1)
    kaug = jnp.pad(kaug, ((0, 0), (0, LANES - 6)))

    x2 = x.reshape(m, d)
    xb = x2.astype(BF16)

    qa = _matmul(xb, w_qa, BF16, scale=LOG2E * A_HEAD_DIM ** -0.5)
    kv = _matmul(xb, w_kv, BF16)
    qidx = _matmul(xb, w_qi, BF16)
    misc = _matmul(xb, w_misc, F32)
    lat = _matmul(xb, w_lat, F32, tn=1152)
    gates = _matmul(xb, w_gate, F32, act="sigmoid")

    tm = 256
    nst = seq // tm
    q_mla = pl.pallas_call(
        functools.partial(_mla_q_kernel, scale=LOG2E * (QK_NOPE + QK_ROPE) ** -0.5),
        grid=(m // tm,),
        in_specs=[pl.BlockSpec((tm, Q_LORA), lambda i: (i, 0)),
                  pl.BlockSpec((1, Q_LORA), lambda i: (0, 0)),
                  pl.BlockSpec((Q_LORA, B_HEADS * QK_PAD), lambda i: (0, 0)),
                  pl.BlockSpec((tm, LANES), lambda i: (i % nst, 0)),
                  pl.BlockSpec((tm, LANES), lambda i: (i % nst, 0))],
        out_specs=pl.BlockSpec((tm, B_HEADS * QK_PAD), lambda i: (i, 0)),
        out_shape=jax.ShapeDtypeStruct((m, B_HEADS * QK_PAD), BF16),
        compiler_params=_cparams(("parallel",)),
        name="mla_q_proj",
    )(lat, row(q_a_norm[0]), wq, cos2, sin2)

    k_mla, v_mla = pl.pallas_call(
        _mla_kv_kernel,
        grid=(m // tm,),
        in_specs=[pl.BlockSpec((tm, KV_LORA), lambda i: (i, 1)),
                  pl.BlockSpec((tm, LANES), lambda i: (i, (Q_LORA + KV_LORA) // LANES)),
                  pl.BlockSpec((1, KV_LORA), lambda i: (0, 0)),
                  pl.BlockSpec((KV_LORA, B_HEADS * QK_NOPE), lambda i: (0, 0)),
                  pl.BlockSpec((KV_LORA, B_HEADS * V_HEAD), lambda i: (0, 0)),
                  pl.BlockSpec((tm, LANES), lambda i: (i % nst, 0)),
                  pl.BlockSpec((tm, LANES), lambda i: (i % nst, 0))],
        out_specs=[pl.BlockSpec((tm, B_HEADS * QK_PAD), lambda i: (i, 0)),
                   pl.BlockSpec((tm, B_HEADS * V_HEAD), lambda i: (i, 0))],
        out_shape=[jax.ShapeDtypeStruct((m, B_HEADS * QK_PAD), BF16),
                   jax.ShapeDtypeStruct((m, B_HEADS * V_HEAD), BF16)],
        compiler_params=_cparams(("parallel",)),
        name="mla_kv_proj",
    )(lat, lat, row(kv_a_norm[0]), wk, wv, cos2, sin2)

    tq, hp = 512, 2
    attn_b = pl.pallas_call(
        functools.partial(_mla_attn_kernel, tq=tq, hp=hp),
        grid=(bsz, B_HEADS // hp, seq // tq),
        in_specs=[pl.BlockSpec((1, tq, hp * QK_PAD), lambda b, h, i: (b, i, h)),
                  pl.BlockSpec((1, seq, hp * QK_PAD), lambda b, h, i: (b, 0, h)),
                  pl.BlockSpec((1, seq, hp * V_HEAD), lambda b, h, i: (b, 0, h))],
        out_specs=pl.BlockSpec((1, tq, hp * V_HEAD), lambda b, h, i: (b, i, h)),
        out_shape=jax.ShapeDtypeStruct((bsz, seq, B_HEADS * V_HEAD), BF16),
        scratch_shapes=[pltpu.VMEM((hp * tq, LANES), F32), pltpu.VMEM((hp * tq, LANES), F32),
                        pltpu.VMEM((hp * tq, V_HEAD), F32)],
        compiler_params=_cparams(("parallel", "parallel", "arbitrary")),
        name="mla_attn",
    )(q_mla.reshape(bsz, seq, -1), k_mla.reshape(bsz, seq, -1), v_mla.reshape(bsz, seq, -1))

    k_a = kv[:, :A_HEAD_DIM].reshape(bsz, seq, A_HEAD_DIM)
    v_a = kv[:, A_HEAD_DIM:].reshape(bsz, seq, A_HEAD_DIM)
    kext = jnp.concatenate([k_a, jnp.broadcast_to(kaug[None], (bsz, seq, LANES))], axis=-1)
    k_idx = misc[:, :IDX_DIM].astype(BF16).reshape(bsz, seq, IDX_DIM)
    zk = jnp.zeros_like(k_idx)
    k_even = jnp.concatenate([k_idx, zk], axis=-1)
    k_odd = jnp.concatenate([zk, k_idx], axis=-1)
    w_idx = misc[:, IDX_DIM:IDX_DIM + IDX_HEADS].reshape(bsz, seq, IDX_HEADS)
    chunk, group = 512, 4
    attn_a = pl.pallas_call(
        functools.partial(_dsa_kernel, q_blk=q_blk, topk=topk, chunk=chunk, group=group),
        grid=(bsz, seq // q_blk),
        in_specs=[pl.BlockSpec((1, q_blk, A_HEADS * A_HEAD_DIM), lambda b, i: (b, i, 0)),
                  pl.BlockSpec((1, q_blk, IDX_HEADS * IDX_DIM), lambda b, i: (b, i, 0)),
                  pl.BlockSpec((1, q_blk, IDX_HEADS), lambda b, i: (b, i, 0)),
                  pl.BlockSpec((1, seq, 2 * A_HEAD_DIM), lambda b, i: (b, 0, 0)),
                  pl.BlockSpec((1, seq, A_HEAD_DIM), lambda b, i: (b, 0, 0)),
                  pl.BlockSpec((1, seq, LANES), lambda b, i: (b, 0, 0)),
                  pl.BlockSpec((1, seq, LANES), lambda b, i: (b, 0, 0)),
                  pl.BlockSpec((A_HEADS * q_blk, LANES), lambda b, i: (0, 0))],
        out_specs=pl.BlockSpec((1, q_blk, A_HEADS * A_HEAD_DIM), lambda b, i: (b, i, 0)),
        out_shape=jax.ShapeDtypeStruct((bsz, seq, A_HEADS * A_HEAD_DIM), BF16),
        scratch_shapes=[pltpu.VMEM((q_blk, seq), jnp.int32),
                        pltpu.VMEM((A_HEADS * q_blk, 2 * A_HEAD_DIM), BF16),
                        pltpu.VMEM((IDX_HEADS // 2 * q_blk, LANES), BF16),
                        pltpu.VMEM((IDX_HEADS, q_blk, LANES), F32),
                        pltpu.VMEM((A_HEADS * q_blk, LANES), F32),
                        pltpu.VMEM((A_HEADS * q_blk, LANES), F32),
                        pltpu.VMEM((A_HEADS * q_blk, A_HEAD_DIM), F32)],
        compiler_params=_cparams(("parallel", "arbitrary")),
        name="dsa_attn",
    )(qa.reshape(bsz, seq, -1), qidx.reshape(bsz, seq, -1), w_idx, kext, v_a, k_even, k_odd, aug_rows)

    tmm, tnm = 1024, 512
    merged = pl.pallas_call(
        _merge_kernel,
        grid=(d // tnm, m // tmm),
        in_specs=[pl.BlockSpec((tmm, d), lambda j, i: (i, 0)),
                  pl.BlockSpec((tmm, d), lambda j, i: (i, 0)),
                  pl.BlockSpec((d, tnm), lambda j, i: (0, j)),
                  pl.BlockSpec((d, tnm), lambda j, i: (0, j)),
                  pl.BlockSpec((tmm, tnm), lambda j, i: (i, j)),
                  pl.BlockSpec((tmm, tnm), lambda j, i: (i, j + d // tnm))],
        out_specs=pl.BlockSpec((tmm, tnm), lambda j, i: (i, j)),
        out_shape=jax.ShapeDtypeStruct((m, d), BF16),
        compiler_params=_cparams(("parallel", "parallel")),
        name="branch_merge",
    )(attn_a.reshape(m, d), attn_b.reshape(m, d), w_branch_a[0].astype(BF16), w_branch_b[0].astype(BF16),
      gates, gates)

    tml = 256
    x1, x1b = pl.pallas_call(
        _proj_ln_kernel,
        grid=(m // tml,),
        in_specs=[pl.BlockSpec((tml, d), lambda i: (i, 0)),
                  pl.BlockSpec((tml, d), lambda i: (i, 0)),
                  pl.BlockSpec((d, d), lambda i: (0, 0)),
                  pl.BlockSpec((1, d), lambda i: (0, 0)),
                  pl.BlockSpec((1, d), lambda i: (0, 0))],
        out_specs=[pl.BlockSpec((tml, d), lambda i: (i, 0)), pl.BlockSpec((tml, d), lambda i: (i, 0))],
        out_shape=[jax.ShapeDtypeStruct((m, d), F32), jax.ShapeDtypeStruct((m, d), BF16)],
        compiler_params=_cparams(("parallel",)),
        name="oproj_ln1",
    )(x2, merged, w_o[0].astype(BF16), row(ln1_g[0]), row(ln1_b[0]))

    tmf, tnf = 1024, 512
    nft = D_FF // tnf
    w_up_b = w_up[0].astype(BF16)
    cw, cb = conv_w[0], row(conv_b[0])
    act = pl.pallas_call(
        functools.partial(_ffn_up_kernel, tm=tmf, seq=seq),
        grid=(nft, m // tmf),
        in_specs=[pl.BlockSpec((tmf, d), lambda j, i: (i, 0)),
                  pl.BlockSpec((8, d), lambda j, i: (jnp.maximum(i * (tmf // 8) - 1, 0), 0)),
                  pl.BlockSpec((d, tnf), lambda j, i: (0, j)),
                  pl.BlockSpec((d, tnf), lambda j, i: (0, j + nft)),
                  pl.BlockSpec((CONV_W, tnf), lambda j, i: (0, j)),
                  pl.BlockSpec((CONV_W, tnf), lambda j, i: (0, j + nft)),
                  pl.BlockSpec((1, tnf), lambda j, i: (0, j)),
                  pl.BlockSpec((1, tnf), lambda j, i: (0, j + nft))],
        out_specs=pl.BlockSpec((tmf, tnf), lambda j, i: (i, j)),
        out_shape=jax.ShapeDtypeStruct((m, D_FF), BF16),
        compiler_params=_cparams(("parallel", "parallel")),
        name="ffn_up_conv",
    )(x1b, x1b, w_up_b, w_up_b, cw, cw, cb, cb)

    tnp = 1024
    ple = pl.pallas_call(
        _ple_kernel,
        grid=(d // tnp, m // tmm),
        in_specs=[pl.BlockSpec((tmm, d), lambda j, i: (i, 0)),
                  pl.BlockSpec((tmm, PLE_DIM), lambda j, i: (i, 0)),
                  pl.BlockSpec((d, tnp), lambda j, i: (0, j)),
                  pl.BlockSpec((1, tnp), lambda j, i: (0, j)),
                  pl.BlockSpec((PLE_DIM, tnp), lambda j, i: (0, j))],
        out_specs=pl.BlockSpec((tmm, tnp), lambda j, i: (i, j)),
        out_shape=jax.ShapeDtypeStruct((m, d), F32),
        compiler_params=_cparams(("parallel", "parallel")),
        name="ple_gate",
    )(x1b, p[0].reshape(m, PLE_DIM).astype(BF16), w_pg[0].astype(BF16), row(b_pg[0]), w_pe[0].astype(BF16))

    tmd, tkd = 512, 512
    out = pl.pallas_call(
        _down_ln_kernel,
        grid=(m // tmd, D_FF // tkd),
        in_specs=[pl.BlockSpec((tmd, d), lambda i, k: (i, 0)),
                  pl.BlockSpec((tmd, tkd), lambda i, k: (i, k)),
                  pl.BlockSpec((tkd, d), lambda i, k: (k, 0)),
                  pl.BlockSpec((tmd, d), lambda i, k: (i, 0)),
                  pl.BlockSpec((1, d), lambda i, k: (0, 0)),
                  pl.BlockSpec((1, d), lambda i, k: (0, 0))],
        out_specs=pl.BlockSpec((tmd, d), lambda i, k: (i, 0)),
        out_shape=jax.ShapeDtypeStruct((m, d), F32),
        scratch_shapes=[pltpu.VMEM((tmd, d), F32)],
        compiler_params=_cparams(("parallel", "arbitrary")),
        name="ffn_down_ln2",
    )(x1, act, w_down[0].astype(BF16), ple, row(ln2_g[0]), row(ln2_b[0]))

    return out.reshape(bsz, seq, d)
```

```python
import functools

import jax
import jax.numpy as jnp
import numpy as np
from jax import lax
from jax.experimental import pallas as pl
from jax.experimental.pallas import tpu as pltpu

F32 = jnp.float32
BF16 = jnp.bfloat16

D_MODEL = 2048
A_HEADS = 16
A_HEAD_DIM = 128
IDX_HEADS = 16
IDX_DIM = 64
INDEX_TOPK_MAX = 256
B_HEADS = 16
Q_LORA = 512
KV_LORA = 512
QK_NOPE = 128
QK_ROPE = 64
V_HEAD = 128
ROPE_THETA = 10000.0
D_FF = 5632
CONV_W = 3
PLE_DIM = 256
LN_EPS = 1e-5
RMS_EPS = 1e-6
NEG = -1e30
DEEPNORM_ALPHA = 2.0 ** 0.25
LOG2E = 1.4426950408889634

LANES = 128
QK_PAD = 256
VE = 2 * LANES
VMEM_LIMIT = 56 * 1024 * 1024
INT_MIN = -2 ** 31
INT_MAX = 2 ** 31 - 1


def _cparams(sem, flags=None):
    return pltpu.CompilerParams(dimension_semantics=sem, vmem_limit_bytes=VMEM_LIMIT, flags=flags)


def _dot(a, b):
    return jnp.dot(a, b, preferred_element_type=F32)


def _dot_nt(a, b):
    return lax.dot_general(a, b, (((1,), (1,)), ((), ())), preferred_element_type=F32)


def _layer_norm(y, g, b):
    mu = jnp.mean(y, axis=-1, keepdims=True)
    d = y - mu
    var = jnp.mean(d * d, axis=-1, keepdims=True)
    return d * lax.rsqrt(var + LN_EPS) * g + b


def _rms_norm(c, g):
    return c * lax.rsqrt(jnp.mean(c * c, axis=-1, keepdims=True) + RMS_EPS) * g


def _rope_lanes(pe, cos2, sin2):
    lane = lax.broadcasted_iota(jnp.int32, pe.shape, 1)
    swapped = jnp.where(lane < QK_ROPE // 2, pltpu.roll(pe, LANES - QK_ROPE // 2, 1),
                        pltpu.roll(pe, QK_ROPE // 2, 1))
    return pe * cos2 + swapped * sin2


def _mm_kernel(x_ref, w_ref, o_ref, *, scale, act):
    y = _dot(x_ref[...], w_ref[...])
    if scale != 1.0:
        y = y * scale
    if act == "sigmoid":
        y = jax.nn.sigmoid(y)
    o_ref[...] = y.astype(o_ref.dtype)


def _matmul(x, w, out_dtype, *, scale=1.0, act=None, tm=1024, tn=1024):
    m, k = x.shape
    n = w.shape[1]
    tn = min(tn, n)
    assert m % tm == 0 and n % tn == 0
    return pl.pallas_call(
        functools.partial(_mm_kernel, scale=scale, act=act),
        grid=(n // tn, m // tm),
        in_specs=[pl.BlockSpec((tm, k), lambda j, i: (i, 0)),
                  pl.BlockSpec((k, tn), lambda j, i: (0, j))],
        out_specs=pl.BlockSpec((tm, tn), lambda j, i: (i, j)),
        out_shape=jax.ShapeDtypeStruct((m, n), out_dtype),
        compiler_params=_cparams(("parallel", "parallel")),
        name="proj_mm",
    )(x, w)


def _mla_q_kernel(c_ref, g_ref, w_ref, cos_ref, sin_ref, o_ref, *, scale):
    y = _rms_norm(c_ref[...], g_ref[...]).astype(BF16)
    q = _dot(y, w_ref[...])
    cos2 = cos_ref[...]
    sin2 = sin_ref[...]
    for h in range(B_HEADS):
        base = h * QK_PAD
        o_ref[:, base:base + QK_NOPE] = (q[:, base:base + QK_NOPE] * scale).astype(BF16)
        pe = q[:, base + QK_NOPE:base + QK_PAD]
        o_ref[:, base + QK_NOPE:base + QK_PAD] = (_rope_lanes(pe, cos2, sin2) * scale).astype(BF16)


def _mla_kv_kernel(c_ref, kpe_ref, g_ref, wk_ref, wv_ref, cos_ref, sin_ref, k_ref, v_ref):
    y = _rms_norm(c_ref[...], g_ref[...]).astype(BF16)
    kn = _dot(y, wk_ref[...])
    v_ref[...] = _dot(y, wv_ref[...]).astype(BF16)
    kpe = _rope_lanes(kpe_ref[...], cos_ref[...], sin_ref[...]).astype(BF16)
    for h in range(B_HEADS):
        base = h * QK_PAD
        k_ref[:, base:base + QK_NOPE] = kn[:, h * QK_NOPE:(h + 1) * QK_NOPE].astype(BF16)
        k_ref[:, base + QK_NOPE:base + QK_PAD] = kpe


def _flash_update(s, v_chunk, m_ref, l_ref, acc_ref, rows):
    nblk = s.shape[1] // LANES
    m_prev = m_ref[rows, :]
    m_new = jnp.maximum(m_prev, jnp.max(s, axis=-1, keepdims=True))
    alpha = jnp.exp2(m_prev - m_new)
    ps = [jnp.exp2(s[:, j * LANES:(j + 1) * LANES] - m_new) for j in range(nblk)]
    p = jnp.concatenate([x.astype(BF16) for x in ps], axis=1)
    if l_ref is None:
        acc_ref[rows, :] = jnp.concatenate([alpha, alpha], axis=1) * acc_ref[rows, :] + _dot(p, v_chunk)
    else:
        l_ref[rows, :] = alpha * l_ref[rows, :] + functools.reduce(lambda a, b: a + b, ps)
        acc_ref[rows, :] = alpha * acc_ref[rows, :] + _dot(p, v_chunk)
    m_ref[rows, :] = m_new


def _flash_finish(l_ref, acc_ref, rows):
    if l_ref is None:
        return acc_ref[rows, 0:LANES] / acc_ref[rows, LANES:VE]
    return acc_ref[rows, :] / jnp.sum(l_ref[rows, :], axis=-1, keepdims=True)


def _mla_attn_kernel(q_ref, k_ref, v_ref, o_ref, m_ref, l_ref, acc_ref, *, tq, hp):
    qi = pl.program_id(2)
    m_ref[...] = jnp.full(m_ref.shape, NEG, F32)
    l_ref[...] = jnp.zeros(l_ref.shape, F32)
    acc_ref[...] = jnp.zeros(acc_ref.shape, F32)

    def step(c, masked):
        off = pl.multiple_of(c * tq, tq)
        for h in range(hp):
            q = q_ref[0, :, h * QK_PAD:(h + 1) * QK_PAD]
            s = _dot_nt(q, k_ref[0, pl.ds(off, tq), h * QK_PAD:(h + 1) * QK_PAD])
            if masked:
                row = lax.broadcasted_iota(jnp.int32, s.shape, 0)
                col = lax.broadcasted_iota(jnp.int32, s.shape, 1)
                s = jnp.where(col <= row, s, NEG)
            _flash_update(s, v_ref[0, pl.ds(off, tq), h * V_HEAD:(h + 1) * V_HEAD],
                          m_ref, l_ref, acc_ref, slice(h * tq, (h + 1) * tq))

    def body(c, carry):
        step(c, False)
        return carry

    lax.fori_loop(0, qi, body, 0)
    step(qi, True)
    for h in range(hp):
        o_ref[0, :, h * V_HEAD:(h + 1) * V_HEAD] = _flash_finish(
            l_ref, acc_ref, slice(h * tq, (h + 1) * tq)).astype(o_ref.dtype)


def _monotone_key(x):
    bits = lax.bitcast_convert_type(x, jnp.int32)
    return bits ^ ((bits >> 31) & INT_MAX)


def _dsa_kernel(qa_ref, qi_ref, w_ref, kext_ref, v_ref, ke_ref, ko_ref, aug_ref, o_ref,
                key_ref, lhs_ref, qs_ref, wb_ref, m_ref, acc_ref,
                *, q_blk, topk, chunk, group):
    qb = pl.program_id(1)
    n_chunks = (qb * q_blk + q_blk + chunk - 1) // chunk
    rep = chunk // LANES
    n_pairs = IDX_HEADS // 2

    for h in range(A_HEADS):
        lhs_ref[h * q_blk:(h + 1) * q_blk, 0:A_HEAD_DIM] = qa_ref[0, :, h * A_HEAD_DIM:(h + 1) * A_HEAD_DIM]
    lhs_ref[:, A_HEAD_DIM:2 * A_HEAD_DIM] = aug_ref[...]
    for j in range(n_pairs):
        qs_ref[j * q_blk:(j + 1) * q_blk, :] = qi_ref[0, :, j * LANES:(j + 1) * LANES]
    for h in range(IDX_HEADS):
        wb_ref[h] = jnp.broadcast_to(w_ref[0, :, h:h + 1], (q_blk, LANES))

    t_col = qb * q_blk + lax.broadcasted_iota(jnp.int32, (q_blk, 1), 0)

    def score_chunk(c, carry):
        rmin, rmax = carry
        off = pl.multiple_of(c * chunk, chunk)
        qs = qs_ref[...]
        se = _dot_nt(qs, ke_ref[0, pl.ds(off, chunk), :])
        so = _dot_nt(qs, ko_ref[0, pl.ds(off, chunk), :])
        acc = jnp.zeros((q_blk, chunk), F32)
        for j in range(n_pairs):
            we = jnp.concatenate([wb_ref[2 * j]] * rep, axis=1)
            wo = jnp.concatenate([wb_ref[2 * j + 1]] * rep, axis=1)
            acc = acc + we * jnp.maximum(se[j * q_blk:(j + 1) * q_blk], 0.0)
            acc = acc + wo * jnp.maximum(so[j * q_blk:(j + 1) * q_blk], 0.0)
        s_pos = off + lax.broadcasted_iota(jnp.int32, (q_blk, chunk), 1)
        causal = s_pos <= t_col
        key_ref[:, pl.ds(off, chunk)] = jnp.where(causal, _monotone_key(acc), INT_MIN)
        lo_c = jnp.where(causal, acc, jnp.inf)
        hi_c = jnp.where(causal, acc, -jnp.inf)
        for j in range(rep):
            rmin = jnp.minimum(rmin, lo_c[:, j * LANES:(j + 1) * LANES])
            rmax = jnp.maximum(rmax, hi_c[:, j * LANES:(j + 1) * LANES])
        return rmin, rmax

    rmin, rmax = lax.fori_loop(0, n_chunks, score_chunk,
                               (jnp.full((q_blk, LANES), jnp.inf, F32), jnp.full((q_blk, LANES), -jnp.inf, F32)))
    s_min = jnp.min(rmin, axis=-1, keepdims=True)
    s_max = jnp.max(rmax, axis=-1, keepdims=True)

    kf = float(topk)

    def count_ge(th):
        thb = jnp.broadcast_to(th, (q_blk, LANES))

        def body(c, cnt):
            off = pl.multiple_of(c * chunk, chunk)
            blk = key_ref[:, pl.ds(off, chunk)]
            for j in range(rep):
                cnt = cnt + jnp.where(blk[:, j * LANES:(j + 1) * LANES] >= thb, 1.0, 0.0)
            return cnt

        cnt = lax.fori_loop(0, n_chunks, body, jnp.zeros((q_blk, LANES), F32))
        return jnp.sum(cnt, axis=-1, keepdims=True)

    n_causal = (t_col + 1).astype(F32)
    few = n_causal <= kf
    lo0 = jnp.where(few, INT_MIN + 1, _monotone_key(s_min) - 1).astype(jnp.int32)
    hi0 = jnp.where(few, INT_MIN + 2, _monotone_key(s_max) + 2).astype(jnp.int32)
    cnt_lo0 = n_causal
    cnt_hi0 = jnp.zeros((q_blk, 1), F32)

    def active(lo, hi, cnt_lo):
        return jnp.logical_and(lo + 1 < hi, cnt_lo != kf)

    def bis_cond(st):
        it, lo, hi, cnt_lo, cnt_hi = st
        n_act = jnp.max(jnp.where(active(lo, hi, cnt_lo), 1, 0))
        return jnp.logical_and(it < 72, n_act > 0)

    def bis_body(st):
        it, lo, hi, cnt_lo, cnt_hi = st
        act = active(lo, hi, cnt_lo)
        mid_b = (lo & hi) + ((lo ^ hi) >> 1)
        v_lo = lax.bitcast_convert_type(_monotone_key(lo), F32)
        v_hi = lax.bitcast_convert_type(_monotone_key(hi), F32)
        frac = (cnt_lo - (kf + 0.5)) / (cnt_lo - cnt_hi)
        mid_i = _monotone_key(v_lo + (v_hi - v_lo) * frac)
        mid_i = jnp.minimum(jnp.maximum(mid_i, lo + 1), hi - 1)
        mid = jnp.where((it & 1) == 0, mid_i, mid_b)
        c = count_ge(mid)
        ge = jnp.logical_and(act, c >= kf)
        lt = jnp.logical_and(act, c < kf)
        return (it + 1, jnp.where(ge, mid, lo), jnp.where(lt, mid, hi),
                jnp.where(ge, c, cnt_lo), jnp.where(lt, c, cnt_hi))

    _, v_key, _, cnt_lo, cnt_hi = lax.while_loop(
        bis_cond, bis_body, (jnp.int32(0), lo0, hi0, cnt_lo0, cnt_hi0))

    tied = jnp.logical_and(cnt_lo > kf, jnp.logical_not(few))
    need = kf - cnt_hi
    n_tied = jnp.max(jnp.where(tied, 1, 0))

    def count_tie_le(jj):
        vb = jnp.broadcast_to(v_key, (q_blk, LANES))
        jb = jnp.broadcast_to(jj, (q_blk, LANES))
        lane = lax.broadcasted_iota(jnp.int32, (q_blk, LANES), 1)

        def body(c, cnt):
            off = pl.multiple_of(c * chunk, chunk)
            blk = key_ref[:, pl.ds(off, chunk)]
            for j in range(rep):
                in_range = jnp.where(off + j * LANES + lane <= jb, 1.0, 0.0)
                cnt = cnt + jnp.where(blk[:, j * LANES:(j + 1) * LANES] == vb, in_range, 0.0)
            return cnt

        cnt = lax.fori_loop(0, n_chunks, body, jnp.zeros((q_blk, LANES), F32))
        return jnp.sum(cnt, axis=-1, keepdims=True)

    def tie_search(_):
        def body(_, st):
            lo, hi = st
            mid = (lo + hi) >> 1
            ok = count_tie_le(mid) >= need
            upd = lo + 1 < hi
            return (jnp.where(jnp.logical_and(upd, jnp.logical_not(ok)), mid, lo),
                    jnp.where(jnp.logical_and(upd, ok), mid, hi))

        lo = jnp.full((q_blk, 1), -1, jnp.int32)
        hi = jnp.broadcast_to(t_col, (q_blk, 1))
        _, hi = lax.fori_loop(0, 14, body, (lo, hi))
        return jnp.where(tied, hi, INT_MAX)

    j_max = lax.cond(n_tied > 0, tie_search, lambda _: jnp.full((q_blk, 1), INT_MAX, jnp.int32), 0)

    m_ref[...] = jnp.full(m_ref.shape, NEG, F32)
    acc_ref[...] = jnp.zeros(acc_ref.shape, F32)
    rows = group * q_blk
    n_groups = A_HEADS // group

    def attn_chunk(c, carry):
        off = pl.multiple_of(c * chunk, chunk)
        keys = key_ref[:, pl.ds(off, chunk)]
        s_pos = off + lax.broadcasted_iota(jnp.int32, (q_blk, chunk), 1)
        sel = jnp.where(keys > v_key, 1, jnp.where(keys == v_key, jnp.where(s_pos <= j_max, 1, 0), 0))
        bias = jnp.where(sel > 0, 0.0, NEG)
        kc = kext_ref[0, pl.ds(off, chunk), :]
        vc = v_ref[0, pl.ds(off, chunk), :]
        for g in range(n_groups):
            r0 = g * rows
            s = _dot_nt(lhs_ref[r0:r0 + rows, :], kc)
            s = (s.reshape(group, q_blk, chunk) + bias[None]).reshape(rows, chunk)
            _flash_update(s, vc, m_ref, None, acc_ref, slice(r0, r0 + rows))
        return carry

    lax.fori_loop(0, n_chunks, attn_chunk, 0)
    for h in range(A_HEADS):
        r0 = h * q_blk
        o_ref[0, :, h * A_HEAD_DIM:(h + 1) * A_HEAD_DIM] = _flash_finish(
            None, acc_ref, slice(r0, r0 + q_blk)).astype(o_ref.dtype)


def _merge_kernel(a_ref, b_ref, wa_ref, wb_ref, g0_ref, g1_ref, o_ref):
    ya = _dot(a_ref[...], wa_ref[...])
    yb = _dot(b_ref[...], wb_ref[...])
    o_ref[...] = (g0_ref[...] * ya + g1_ref[...] * yb).astype(o_ref.dtype)


def _proj_ln_kernel(x_ref, a_ref, w_ref, g_ref, b_ref, o_ref, obf_ref):
    y = DEEPNORM_ALPHA * x_ref[...] + _dot(a_ref[...], w_ref[...])
    out = _layer_norm(y, g_ref[...], b_ref[...])
    o_ref[...] = out
    obf_ref[...] = out.astype(BF16)


def _ffn_up_kernel(x_ref, xp_ref, wg_ref, wv_ref, cwg_ref, cwv_ref, cbg_ref, cbv_ref, o_ref, *, tm, seq):
    i = pl.program_id(1)
    first = (i * tm) % seq == 0
    x = x_ref[...]
    xp = xp_ref[...]
    row = lax.broadcasted_iota(jnp.int32, (tm, 1), 0)

    def conv(w_ref, cw_ref, cb_ref):
        h = _dot(x, w_ref[...])
        hp = jnp.where(first, 0.0, _dot(xp, w_ref[...]))
        hm1 = jnp.where(row == 0, hp[7:8, :], pltpu.roll(h, 1, 0))
        hm2 = jnp.where(row == 0, hp[6:7, :], jnp.where(row == 1, hp[7:8, :], pltpu.roll(h, 2, 0)))
        return cw_ref[0:1, :] * hm2 + cw_ref[1:2, :] * hm1 + cw_ref[2:3, :] * h + cb_ref[...]

    hg = conv(wg_ref, cwg_ref, cbg_ref)
    hv = conv(wv_ref, cwv_ref, cbv_ref)
    o_ref[...] = (hg * jax.nn.sigmoid(hg) * hv).astype(o_ref.dtype)


def _ple_kernel(x_ref, p_ref, wpg_ref, bpg_ref, wpe_ref, o_ref):
    gate = jax.nn.sigmoid(_dot(x_ref[...], wpg_ref[...]) + bpg_ref[...])
    o_ref[...] = gate * _dot(p_ref[...], wpe_ref[...])


def _down_ln_kernel(x_ref, a_ref, w_ref, ple_ref, g_ref, b_ref, o_ref, acc_ref):
    k = pl.program_id(1)

    @pl.when(k == 0)
    def _():
        acc_ref[...] = DEEPNORM_ALPHA * x_ref[...] + ple_ref[...]

    acc_ref[...] += _dot(a_ref[...], w_ref[...])

    @pl.when(k == pl.num_programs(1) - 1)
    def _():
        o_ref[...] = _layer_norm(acc_ref[...], g_ref[...], b_ref[...])


def _split_bf16(x, parts):
    out = []
    rem = x
    for _ in range(parts):
        piece = rem.astype(BF16)
        out.append(piece)
        rem = rem - piece.astype(F32)
    return out


def kernel(x, p, w_in, q_a_norm, kv_a_norm, w_uq, w_uk, w_uv, w_branch_a, w_branch_b, w_o,
           ln1_g, ln1_b, w_up, conv_w, conv_b, w_down, w_pg, b_pg, w_pe, ln2_g, ln2_b):
    bsz, seq, d = x.shape
    m = bsz * seq
    topk = min(INDEX_TOPK_MAX, seq // 4)
    row = lambda a: a.reshape(1, -1)

    wi = w_in[0]
    o_qa, o_ka, o_va = 0, 2048, 2176
    o_qi, o_ki, o_wi, o_cq, o_ckv, o_kpe, o_gate = 2304, 3328, 3392, 3408, 3920, 4432, 4496
    w_qa = wi[:, o_qa:o_ka].astype(BF16)
    w_kv = wi[:, o_ka:o_qi].astype(BF16)
    w_qi = wi[:, o_qi:o_ki].astype(BF16)
    w_misc = jnp.pad(wi[:, o_ki:o_cq], ((0, 0), (0, LANES - (o_cq - o_ki)))).astype(BF16)
    w_lat = jnp.pad(wi[:, o_cq:o_gate], ((0, 0), (0, LANES - QK_ROPE))).astype(BF16)
    w_gate = wi[:, o_gate:].astype(BF16)

    wq = jnp.pad(w_uq[0], ((0, 0), (0, 0), (0, QK_PAD - QK_NOPE - QK_ROPE)))
    wq = wq.reshape(Q_LORA, B_HEADS * QK_PAD).astype(BF16)
    wk = w_uk[0].reshape(KV_LORA, B_HEADS * QK_NOPE).astype(BF16)
    wv = w_uv[0].reshape(KV_LORA, B_HEADS * V_HEAD).astype(BF16)

    pos = jnp.arange(seq, dtype=F32)
    inv_freq = jnp.power(ROPE_THETA, -jnp.arange(0, QK_ROPE, 2, dtype=F32) / QK_ROPE)
    ang = pos[:, None] * inv_freq[None, :]
    cos, sin = jnp.cos(ang), jnp.sin(ang)
    zpad = jnp.zeros((seq, LANES - QK_ROPE), F32)
    cos2 = jnp.concatenate([cos, cos, zpad], axis=1)
    sin2 = jnp.concatenate([-sin, sin, zpad], axis=1)

    slopes = LOG2E * jnp.exp2(-8.0 * jnp.arange(1, A_HEADS + 1, dtype=F32) / A_HEADS)
    sl = _split_bf16(slopes, 3)
    aug = jnp.stack([s_ * 64.0 for s_ in sl] + sl, axis=1).astype(BF16)
    aug = jnp.pad(aug, ((0, 0), (0, LANES - 6)))
    q_blk = 128
    aug_rows = jnp.repeat(aug, q_blk, axis=0)
    s_int = jnp.arange(seq, dtype=jnp.int32)
    s_hi = (s_int // 64).astype(BF16)
    s_lo = (s_int % 64).astype(BF16)
    kaug = jnp.stack([s_hi, s_hi, s_hi, s_lo, s_lo, s_lo], axis=1)
    kaug = jnp.pad(kaug, ((0, 0), (0, LANES - 6)))

    x2 = x.reshape(m, d)
    xb = x2.astype(BF16)

    qa = _matmul(xb, w_qa, BF16, scale=LOG2E * A_HEAD_DIM ** -0.5)
    kv = _matmul(xb, w_kv, BF16)
    qidx = _matmul(xb, w_qi, BF16)
    misc = _matmul(xb, w_misc, F32)
    lat = _matmul(xb, w_lat, F32, tn=1152)
    gates = _matmul(xb, w_gate, F32, act="sigmoid")

    tm = 256
    nst = seq // tm
    q_mla = pl.pallas_call(
        functools.partial(_mla_q_kernel, scale=LOG2E * (QK_NOPE + QK_ROPE) ** -0.5),
        grid=(m // tm,),
        in_specs=[pl.BlockSpec((tm, Q_LORA), lambda i: (i, 0)),
                  pl.BlockSpec((1, Q_LORA), lambda i: (0, 0)),
                  pl.BlockSpec((Q_LORA, B_HEADS * QK_PAD), lambda i: (0, 0)),
                  pl.BlockSpec((tm, LANES), lambda i: (i % nst, 0)),
                  pl.BlockSpec((tm, LANES), lambda i: (i % nst, 0))],
        out_specs=pl.BlockSpec((tm, B_HEADS * QK_PAD), lambda i: (i, 0)),
        out_shape=jax.ShapeDtypeStruct((m, B_HEADS * QK_PAD), BF16),
        compiler_params=_cparams(("parallel",)),
        name="mla_q_proj",
    )(lat, row(q_a_norm[0]), wq, cos2, sin2)

    k_mla, v_mla = pl.pallas_call(
        _mla_kv_kernel,
        grid=(m // tm,),
        in_specs=[pl.BlockSpec((tm, KV_LORA), lambda i: (i, 1)),
                  pl.BlockSpec((tm, LANES), lambda i: (i, (Q_LORA + KV_LORA) // LANES)),
                  pl.BlockSpec((1, KV_LORA), lambda i: (0, 0)),
                  pl.BlockSpec((KV_LORA, B_HEADS * QK_NOPE), lambda i: (0, 0)),
                  pl.BlockSpec((KV_LORA, B_HEADS * V_HEAD), lambda i: (0, 0)),
                  pl.BlockSpec((tm, LANES), lambda i: (i % nst, 0)),
                  pl.BlockSpec((tm, LANES), lambda i: (i % nst, 0))],
        out_specs=[pl.BlockSpec((tm, B_HEADS * QK_PAD), lambda i: (i, 0)),
                   pl.BlockSpec((tm, B_HEADS * V_HEAD), lambda i: (i, 0))],
        out_shape=[jax.ShapeDtypeStruct((m, B_HEADS * QK_PAD), BF16),
                   jax.ShapeDtypeStruct((m, B_HEADS * V_HEAD), BF16)],
        compiler_params=_cparams(("parallel",)),
        name="mla_kv_proj",
    )(lat, lat, row(kv_a_norm[0]), wk, wv, cos2, sin2)

    tq, hp = 512, 2
    attn_b = pl.pallas_call(
        functools.partial(_mla_attn_kernel, tq=tq, hp=hp),
        grid=(bsz, B_HEADS // hp, seq // tq),
        in_specs=[pl.BlockSpec((1, tq, hp * QK_PAD), lambda b, h, i: (b, i, h)),
                  pl.BlockSpec((1, seq, hp * QK_PAD), lambda b, h, i: (b, 0, h)),
                  pl.BlockSpec((1, seq, hp * V_HEAD), lambda b, h, i: (b, 0, h))],
        out_specs=pl.BlockSpec((1, tq, hp * V_HEAD), lambda b, h, i: (b, i, h)),
        out_shape=jax.ShapeDtypeStruct((bsz, seq, B_HEADS * V_HEAD), BF16),
        scratch_shapes=[pltpu.VMEM((hp * tq, LANES), F32), pltpu.VMEM((hp * tq, LANES), F32),
                        pltpu.VMEM((hp * tq, V_HEAD), F32)],
        compiler_params=_cparams(("parallel", "parallel", "arbitrary")),
        name="mla_attn",
    )(q_mla.reshape(bsz, seq, -1), k_mla.reshape(bsz, seq, -1), v_mla.reshape(bsz, seq, -1))

    k_a = kv[:, :A_HEAD_DIM].reshape(bsz, seq, A_HEAD_DIM)
    v_a = kv[:, A_HEAD_DIM:].reshape(bsz, seq, A_HEAD_DIM)
    v_ext = jnp.concatenate([v_a, jnp.ones_like(v_a)], axis=-1)
    kext = jnp.concatenate([k_a, jnp.broadcast_to(kaug[None], (bsz, seq, LANES))], axis=-1)
    k_idx = misc[:, :IDX_DIM].astype(BF16).reshape(bsz, seq, IDX_DIM)
    zk = jnp.zeros_like(k_idx)
    k_even = jnp.concatenate([k_idx, zk], axis=-1)
    k_odd = jnp.concatenate([zk, k_idx], axis=-1)
    w_idx = misc[:, IDX_DIM:IDX_DIM + IDX_HEADS].reshape(bsz, seq, IDX_HEADS)
    chunk, group = 512, 4
    attn_a = pl.pallas_call(
        functools.partial(_dsa_kernel, q_blk=q_blk, topk=topk, chunk=chunk, group=group),
        grid=(bsz, seq // q_blk),
        in_specs=[pl.BlockSpec((1, q_blk, A_HEADS * A_HEAD_DIM), lambda b, i: (b, i, 0)),
                  pl.BlockSpec((1, q_blk, IDX_HEADS * IDX_DIM), lambda b, i: (b, i, 0)),
                  pl.BlockSpec((1, q_blk, IDX_HEADS), lambda b, i: (b, i, 0)),
                  pl.BlockSpec((1, seq, 2 * A_HEAD_DIM), lambda b, i: (b, 0, 0)),
                  pl.BlockSpec((1, seq, VE), lambda b, i: (b, 0, 0)),
                  pl.BlockSpec((1, seq, LANES), lambda b, i: (b, 0, 0)),
                  pl.BlockSpec((1, seq, LANES), lambda b, i: (b, 0, 0)),
                  pl.BlockSpec((A_HEADS * q_blk, LANES), lambda b, i: (0, 0))],
        out_specs=pl.BlockSpec((1, q_blk, A_HEADS * A_HEAD_DIM), lambda b, i: (b, i, 0)),
        out_shape=jax.ShapeDtypeStruct((bsz, seq, A_HEADS * A_HEAD_DIM), BF16),
        scratch_shapes=[pltpu.VMEM((q_blk, seq), jnp.int32),
                        pltpu.VMEM((A_HEADS * q_blk, 2 * A_HEAD_DIM), BF16),
                        pltpu.VMEM((IDX_HEADS // 2 * q_blk, LANES), BF16),
                        pltpu.VMEM((IDX_HEADS, q_blk, LANES), F32),
                        pltpu.VMEM((A_HEADS * q_blk, LANES), F32),
                        pltpu.VMEM((A_HEADS * q_blk, VE), F32)],
        compiler_params=_cparams(("parallel", "arbitrary")),
        name="dsa_attn",
    )(qa.reshape(bsz, seq, -1), qidx.reshape(bsz, seq, -1), w_idx, kext, v_ext, k_even, k_odd, aug_rows)

    tmm, tnm = 1024, 512
    merged = pl.pallas_call(
        _merge_kernel,
        grid=(d // tnm, m // tmm),
        in_specs=[pl.BlockSpec((tmm, d), lambda j, i: (i, 0)),
                  pl.BlockSpec((tmm, d), lambda j, i: (i, 0)),
                  pl.BlockSpec((d, tnm), lambda j, i: (0, j)),
                  pl.BlockSpec((d, tnm), lambda j, i: (0, j)),
                  pl.BlockSpec((tmm, tnm), lambda j, i: (i, j)),
                  pl.BlockSpec((tmm, tnm), lambda j, i: (i, j + d // tnm))],
        out_specs=pl.BlockSpec((tmm, tnm), lambda j, i: (i, j)),
        out_shape=jax.ShapeDtypeStruct((m, d), BF16),
        compiler_params=_cparams(("parallel", "parallel")),
        name="branch_merge",
    )(attn_a.reshape(m, d), attn_b.reshape(m, d), w_branch_a[0].astype(BF16), w_branch_b[0].astype(BF16),
      gates, gates)

    tml = 256
    x1, x1b = pl.pallas_call(
        _proj_ln_kernel,
        grid=(m // tml,),
        in_specs=[pl.BlockSpec((tml, d), lambda i: (i, 0)),
                  pl.BlockSpec((tml, d), lambda i: (i, 0)),
                  pl.BlockSpec((d, d), lambda i: (0, 0)),
                  pl.BlockSpec((1, d), lambda i: (0, 0)),
                  pl.BlockSpec((1, d), lambda i: (0, 0))],
        out_specs=[pl.BlockSpec((tml, d), lambda i: (i, 0)), pl.BlockSpec((tml, d), lambda i: (i, 0))],
        out_shape=[jax.ShapeDtypeStruct((m, d), F32), jax.ShapeDtypeStruct((m, d), BF16)],
        compiler_params=_cparams(("parallel",)),
        name="oproj_ln1",
    )(x2, merged, w_o[0].astype(BF16), row(ln1_g[0]), row(ln1_b[0]))

    tmf, tnf = 1024, 512
    nft = D_FF // tnf
    w_up_b = w_up[0].astype(BF16)
    cw, cb = conv_w[0], row(conv_b[0])
    act = pl.pallas_call(
        functools.partial(_ffn_up_kernel, tm=tmf, seq=seq),
        grid=(nft, m // tmf),
        in_specs=[pl.BlockSpec((tmf, d), lambda j, i: (i, 0)),
                  pl.BlockSpec((8, d), lambda j, i: (jnp.maximum(i * (tmf // 8) - 1, 0), 0)),
                  pl.BlockSpec((d, tnf), lambda j, i: (0, j)),
                  pl.BlockSpec((d, tnf), lambda j, i: (0, j + nft)),
                  pl.BlockSpec((CONV_W, tnf), lambda j, i: (0, j)),
                  pl.BlockSpec((CONV_W, tnf), lambda j, i: (0, j + nft)),
                  pl.BlockSpec((1, tnf), lambda j, i: (0, j)),
                  pl.BlockSpec((1, tnf), lambda j, i: (0, j + nft))],
        out_specs=pl.BlockSpec((tmf, tnf), lambda j, i: (i, j)),
        out_shape=jax.ShapeDtypeStruct((m, D_FF), BF16),
        compiler_params=_cparams(("parallel", "parallel")),
        name="ffn_up_conv",
    )(x1b, x1b, w_up_b, w_up_b, cw, cw, cb, cb)

    tnp = 1024
    ple = pl.pallas_call(
        _ple_kernel,
        grid=(d // tnp, m // tmm),
        in_specs=[pl.BlockSpec((tmm, d), lambda j, i: (i, 0)),
                  pl.BlockSpec((tmm, PLE_DIM), lambda j, i: (i, 0)),
                  pl.BlockSpec((d, tnp), lambda j, i: (0, j)),
                  pl.BlockSpec((1, tnp), lambda j, i: (0, j)),
                  pl.BlockSpec((PLE_DIM, tnp), lambda j, i: (0, j))],
        out_specs=pl.BlockSpec((tmm, tnp), lambda j, i: (i, j)),
        out_shape=jax.ShapeDtypeStruct((m, d), F32),
        compiler_params=_cparams(("parallel", "parallel")),
        name="ple_gate",
    )(x1b, p[0].reshape(m, PLE_DIM).astype(BF16), w_pg[0].astype(BF16), row(b_pg[0]), w_pe[0].astype(BF16))

    tmd, tkd = 512, 512
    out = pl.pallas_call(
        _down_ln_kernel,
        grid=(m // tmd, D_FF // tkd),
        in_specs=[pl.BlockSpec((tmd, d), lambda i, k: (i, 0)),
                  pl.BlockSpec((tmd, tkd), lambda i, k: (i, k)),
                  pl.BlockSpec((tkd, d), lambda i, k: (k, 0)),
                  pl.BlockSpec((tmd, d), lambda i, k: (i, 0)),
                  pl.BlockSpec((1, d), lambda i, k: (0, 0)),
                  pl.BlockSpec((1, d), lambda i, k: (0, 0))],
        out_specs=pl.BlockSpec((tmd, d), lambda i, k: (i, 0)),
        out_shape=jax.ShapeDtypeStruct((m, d), F32),
        scratch_shapes=[pltpu.VMEM((tmd, d), F32)],
        compiler_params=_cparams(("parallel", "arbitrary")),
        name="ffn_down_ln2",
    )(x1, act, w_down[0].astype(BF16), ple, row(ln2_g[0]), row(ln2_b[0]))

    return out.reshape(bsz, seq, d)
```

```python
import functools

import jax
import jax.numpy as jnp
import numpy as np
from jax import lax
from jax.experimental import pallas as pl
from jax.experimental.pallas import tpu as pltpu

F32 = jnp.float32
BF16 = jnp.bfloat16

D_MODEL = 2048
A_HEADS = 16
A_HEAD_DIM = 128
IDX_HEADS = 16
IDX_DIM = 64
INDEX_TOPK_MAX = 256
B_HEADS = 16
Q_LORA = 512
KV_LORA = 512
QK_NOPE = 128
QK_ROPE = 64
V_HEAD = 128
ROPE_THETA = 10000.0
D_FF = 5632
CONV_W = 3
PLE_DIM = 256
LN_EPS = 1e-5
RMS_EPS = 1e-6
NEG = -1e30
DEEPNORM_ALPHA = 2.0 ** 0.25
LOG2E = 1.4426950408889634

LANES = 128
QK_PAD = 256
VE = 2 * LANES
VMEM_LIMIT = 56 * 1024 * 1024
INT_MIN = -2 ** 31
INT_MAX = 2 ** 31 - 1


def _cparams(sem, flags=None):
    return pltpu.CompilerParams(dimension_semantics=sem, vmem_limit_bytes=VMEM_LIMIT, flags=flags)


def _dot(a, b):
    return jnp.dot(a, b, preferred_element_type=F32)


def _dot_nt(a, b):
    return lax.dot_general(a, b, (((1,), (1,)), ((), ())), preferred_element_type=F32)


def _layer_norm(y, g, b):
    mu = jnp.mean(y, axis=-1, keepdims=True)
    d = y - mu
    var = jnp.mean(d * d, axis=-1, keepdims=True)
    return d * lax.rsqrt(var + LN_EPS) * g + b


def _rms_norm(c, g):
    return c * lax.rsqrt(jnp.mean(c * c, axis=-1, keepdims=True) + RMS_EPS) * g


def _rope_lanes(pe, cos2, sin2):
    lane = lax.broadcasted_iota(jnp.int32, pe.shape, 1)
    swapped = jnp.where(lane < QK_ROPE // 2, pltpu.roll(pe, LANES - QK_ROPE // 2, 1),
                        pltpu.roll(pe, QK_ROPE // 2, 1))
    return pe * cos2 + swapped * sin2


def _mm_kernel(x_ref, w_ref, o_ref, *, scale, act):
    y = _dot(x_ref[...], w_ref[...])
    if scale != 1.0:
        y = y * scale
    if act == "sigmoid":
        y = jax.nn.sigmoid(y)
    o_ref[...] = y.astype(o_ref.dtype)


def _matmul(x, w, out_dtype, *, scale=1.0, act=None, tm=1024, tn=1024):
    m, k = x.shape
    n = w.shape[1]
    tn = min(tn, n)
    assert m % tm == 0 and n % tn == 0
    return pl.pallas_call(
        functools.partial(_mm_kernel, scale=scale, act=act),
        grid=(n // tn, m // tm),
        in_specs=[pl.BlockSpec((tm, k), lambda j, i: (i, 0)),
                  pl.BlockSpec((k, tn), lambda j, i: (0, j))],
        out_specs=pl.BlockSpec((tm, tn), lambda j, i: (i, j)),
        out_shape=jax.ShapeDtypeStruct((m, n), out_dtype),
        compiler_params=_cparams(("parallel", "parallel")),
        name="proj_mm",
    )(x, w)


def _mla_q_kernel(c_ref, g_ref, w_ref, cos_ref, sin_ref, o_ref, *, scale):
    y = _rms_norm(c_ref[...], g_ref[...]).astype(BF16)
    q = _dot(y, w_ref[...])
    cos2 = cos_ref[...]
    sin2 = sin_ref[...]
    for h in range(B_HEADS):
        base = h * QK_PAD
        o_ref[:, base:base + QK_NOPE] = (q[:, base:base + QK_NOPE] * scale).astype(BF16)
        pe = q[:, base + QK_NOPE:base + QK_PAD]
        o_ref[:, base + QK_NOPE:base + QK_PAD] = (_rope_lanes(pe, cos2, sin2) * scale).astype(BF16)


def _mla_kv_kernel(c_ref, kpe_ref, g_ref, wk_ref, wv_ref, cos_ref, sin_ref, k_ref, v_ref):
    y = _rms_norm(c_ref[...], g_ref[...]).astype(BF16)
    kn = _dot(y, wk_ref[...])
    v_ref[...] = _dot(y, wv_ref[...]).astype(BF16)
    kpe = _rope_lanes(kpe_ref[...], cos_ref[...], sin_ref[...]).astype(BF16)
    for h in range(B_HEADS):
        base = h * QK_PAD
        k_ref[:, base:base + QK_NOPE] = kn[:, h * QK_NOPE:(h + 1) * QK_NOPE].astype(BF16)
        k_ref[:, base + QK_NOPE:base + QK_PAD] = kpe


def _flash_update(s, v_chunk, m_ref, l_ref, acc_ref, rows):
    nblk = s.shape[1] // LANES
    m_prev = m_ref[rows, :]
    m_new = jnp.maximum(m_prev, jnp.max(s, axis=-1, keepdims=True))
    alpha = jnp.exp2(m_prev - m_new)
    ps = [jnp.exp2(s[:, j * LANES:(j + 1) * LANES] - m_new) for j in range(nblk)]
    p = jnp.concatenate([x.astype(BF16) for x in ps], axis=1)
    if l_ref is None:
        acc_ref[rows, :] = jnp.concatenate([alpha, alpha], axis=1) * acc_ref[rows, :] + _dot(p, v_chunk)
    else:
        l_ref[rows, :] = alpha * l_ref[rows, :] + functools.reduce(lambda a, b: a + b, ps)
        acc_ref[rows, :] = alpha * acc_ref[rows, :] + _dot(p, v_chunk)
    m_ref[rows, :] = m_new


def _flash_finish(l_ref, acc_ref, rows):
    if l_ref is None:
        return acc_ref[rows, 0:LANES] / acc_ref[rows, LANES:VE]
    return acc_ref[rows, :] / jnp.sum(l_ref[rows, :], axis=-1, keepdims=True)


def _mla_attn_kernel(q_ref, k_ref, v_ref, o_ref, m_ref, l_ref, acc_ref, *, tq, tk, hp):
    qi = pl.program_id(2)
    m_ref[...] = jnp.full(m_ref.shape, NEG, F32)
    l_ref[...] = jnp.zeros(l_ref.shape, F32)
    acc_ref[...] = jnp.zeros(acc_ref.shape, F32)

    def step(c, diag):
        off = pl.multiple_of(c * tk, tk)
        for h in range(hp):
            q = q_ref[0, :, h * QK_PAD:(h + 1) * QK_PAD]
            s = _dot_nt(q, k_ref[0, pl.ds(off, tk), h * QK_PAD:(h + 1) * QK_PAD])
            if diag is not None:
                row = lax.broadcasted_iota(jnp.int32, s.shape, 0)
                col = lax.broadcasted_iota(jnp.int32, s.shape, 1) + diag * tk
                s = jnp.where(col <= row, s, NEG)
            _flash_update(s, v_ref[0, pl.ds(off, tk), h * V_HEAD:(h + 1) * V_HEAD],
                          m_ref, l_ref, acc_ref, slice(h * tq, (h + 1) * tq))

    def body(c, carry):
        step(c, None)
        return carry

    n_diag = tq // tk
    lax.fori_loop(0, qi * n_diag, body, 0)
    for d in range(n_diag):
        step(qi * n_diag + d, d)
    for h in range(hp):
        o_ref[0, :, h * V_HEAD:(h + 1) * V_HEAD] = _flash_finish(
            l_ref, acc_ref, slice(h * tq, (h + 1) * tq)).astype(o_ref.dtype)


def _monotone_key(x):
    bits = lax.bitcast_convert_type(x, jnp.int32)
    return bits ^ ((bits >> 31) & INT_MAX)


def _dsa_kernel(qa_ref, qi_ref, w_ref, kext_ref, v_ref, ke_ref, ko_ref, aug_ref, o_ref,
                key_ref, lhs_ref, qs_ref, wb_ref, m_ref, l_ref, acc_ref,
                *, q_blk, topk, chunk, group, ones, blind):
    if ones:
        l_ref = None
    qb = pl.program_id(1)
    n_chunks = (qb * q_blk + q_blk + chunk - 1) // chunk
    rep = chunk // LANES
    n_pairs = IDX_HEADS // 2

    for h in range(A_HEADS):
        lhs_ref[h * q_blk:(h + 1) * q_blk, 0:A_HEAD_DIM] = qa_ref[0, :, h * A_HEAD_DIM:(h + 1) * A_HEAD_DIM]
    lhs_ref[:, A_HEAD_DIM:2 * A_HEAD_DIM] = aug_ref[...]
    for j in range(n_pairs):
        qs_ref[j * q_blk:(j + 1) * q_blk, :] = qi_ref[0, :, j * LANES:(j + 1) * LANES]
    for h in range(IDX_HEADS):
        wb_ref[h] = jnp.broadcast_to(w_ref[0, :, h:h + 1], (q_blk, LANES))

    t_col = qb * q_blk + lax.broadcasted_iota(jnp.int32, (q_blk, 1), 0)

    def score_chunk(c, carry):
        off = pl.multiple_of(c * chunk, chunk)
        qs = qs_ref[...]
        se = _dot_nt(qs, ke_ref[0, pl.ds(off, chunk), :])
        so = _dot_nt(qs, ko_ref[0, pl.ds(off, chunk), :])
        acc = jnp.zeros((q_blk, chunk), F32)
        for j in range(n_pairs):
            we = jnp.concatenate([wb_ref[2 * j]] * rep, axis=1)
            wo = jnp.concatenate([wb_ref[2 * j + 1]] * rep, axis=1)
            acc = acc + we * jnp.maximum(se[j * q_blk:(j + 1) * q_blk], 0.0)
            acc = acc + wo * jnp.maximum(so[j * q_blk:(j + 1) * q_blk], 0.0)
        s_pos = off + lax.broadcasted_iota(jnp.int32, (q_blk, chunk), 1)
        key_ref[:, pl.ds(off, chunk)] = jnp.where(s_pos <= t_col, _monotone_key(acc), INT_MIN)
        return carry

    lax.fori_loop(0, n_chunks, score_chunk, 0)

    kf = float(topk)

    def count_ge(th):
        thb = jnp.broadcast_to(th, (q_blk, LANES))

        def body(c, cnt):
            off = pl.multiple_of(c * chunk, chunk)
            blk = key_ref[:, pl.ds(off, chunk)]
            for j in range(rep):
                cnt = cnt + jnp.where(blk[:, j * LANES:(j + 1) * LANES] >= thb, 1.0, 0.0)
            return cnt

        cnt = lax.fori_loop(0, n_chunks, body, jnp.zeros((q_blk, LANES), F32))
        return jnp.sum(cnt, axis=-1, keepdims=True)

    n_causal = (t_col + 1).astype(F32)
    few = n_causal <= kf
    lo0 = jnp.full((q_blk, 1), INT_MIN + 1, jnp.int32)
    hi0 = jnp.where(few, INT_MIN + 2, INT_MAX).astype(jnp.int32)
    cnt_lo0 = n_causal
    cnt_hi0 = jnp.zeros((q_blk, 1), F32)

    def active(lo, hi, cnt_lo):
        return jnp.logical_and(lo + 1 < hi, cnt_lo != kf)

    def bis_step(st):
        lo, hi, cnt_lo, cnt_hi = st
        act = active(lo, hi, cnt_lo)
        mid = (lo & hi) + ((lo ^ hi) >> 1)
        c = count_ge(mid)
        ge = jnp.logical_and(act, c >= kf)
        lt = jnp.logical_and(act, c < kf)
        return (jnp.where(ge, mid, lo), jnp.where(lt, mid, hi), jnp.where(ge, c, cnt_lo), jnp.where(lt, c, cnt_hi))

    def bis_cond(st):
        it, lo, hi, cnt_lo, cnt_hi = st
        n_act = jnp.max(jnp.where(active(lo, hi, cnt_lo), 1, 0))
        return jnp.logical_and(it < 32 - blind + 2, n_act > 0)

    st = lax.fori_loop(0, blind, lambda _, s_: bis_step(s_), (lo0, hi0, cnt_lo0, cnt_hi0))
    _, v_key, _, cnt_lo, cnt_hi = lax.while_loop(
        bis_cond, lambda s_: (s_[0] + 1,) + bis_step(s_[1:]), (jnp.int32(0),) + st)

    tied = jnp.logical_and(cnt_lo > kf, jnp.logical_not(few))
    need = kf - cnt_hi
    n_tied = jnp.max(jnp.where(tied, 1, 0))

    def count_tie_le(jj):
        vb = jnp.broadcast_to(v_key, (q_blk, LANES))
        jb = jnp.broadcast_to(jj, (q_blk, LANES))
        lane = lax.broadcasted_iota(jnp.int32, (q_blk, LANES), 1)

        def body(c, cnt):
            off = pl.multiple_of(c * chunk, chunk)
            blk = key_ref[:, pl.ds(off, chunk)]
            for j in range(rep):
                in_range = jnp.where(off + j * LANES + lane <= jb, 1.0, 0.0)
                cnt = cnt + jnp.where(blk[:, j * LANES:(j + 1) * LANES] == vb, in_range, 0.0)
            return cnt

        cnt = lax.fori_loop(0, n_chunks, body, jnp.zeros((q_blk, LANES), F32))
        return jnp.sum(cnt, axis=-1, keepdims=True)

    def tie_search(_):
        def body(_, st):
            lo, hi = st
            mid = (lo + hi) >> 1
            ok = count_tie_le(mid) >= need
            upd = lo + 1 < hi
            return (jnp.where(jnp.logical_and(upd, jnp.logical_not(ok)), mid, lo),
                    jnp.where(jnp.logical_and(upd, ok), mid, hi))

        lo = jnp.full((q_blk, 1), -1, jnp.int32)
        hi = jnp.broadcast_to(t_col, (q_blk, 1))
        _, hi = lax.fori_loop(0, 14, body, (lo, hi))
        return jnp.where(tied, hi, INT_MAX)

    j_max = lax.cond(n_tied > 0, tie_search, lambda _: jnp.full((q_blk, 1), INT_MAX, jnp.int32), 0)

    m_ref[...] = jnp.full(m_ref.shape, NEG, F32)
    acc_ref[...] = jnp.zeros(acc_ref.shape, F32)
    if l_ref is not None:
        l_ref[...] = jnp.zeros(l_ref.shape, F32)
    rows = group * q_blk
    n_groups = A_HEADS // group

    def attn_chunk(c, carry):
        off = pl.multiple_of(c * chunk, chunk)
        keys = key_ref[:, pl.ds(off, chunk)]
        s_pos = off + lax.broadcasted_iota(jnp.int32, (q_blk, chunk), 1)
        sel = jnp.where(keys > v_key, 1, jnp.where(keys == v_key, jnp.where(s_pos <= j_max, 1, 0), 0))
        bias = jnp.where(sel > 0, 0.0, NEG)
        kc = kext_ref[0, pl.ds(off, chunk), :]
        vc = v_ref[0, pl.ds(off, chunk), :]
        for g in range(n_groups):
            r0 = g * rows
            s = _dot_nt(lhs_ref[r0:r0 + rows, :], kc)
            s = (s.reshape(group, q_blk, chunk) + bias[None]).reshape(rows, chunk)
            _flash_update(s, vc, m_ref, l_ref, acc_ref, slice(r0, r0 + rows))
        return carry

    lax.fori_loop(0, n_chunks, attn_chunk, 0)
    for h in range(A_HEADS):
        r0 = h * q_blk
        o_ref[0, :, h * A_HEAD_DIM:(h + 1) * A_HEAD_DIM] = _flash_finish(
            l_ref, acc_ref, slice(r0, r0 + q_blk)).astype(o_ref.dtype)


def _merge_kernel(a_ref, b_ref, wa_ref, wb_ref, g0_ref, g1_ref, o_ref):
    ya = _dot(a_ref[...], wa_ref[...])
    yb = _dot(b_ref[...], wb_ref[...])
    o_ref[...] = (g0_ref[...] * ya + g1_ref[...] * yb).astype(o_ref.dtype)


def _proj_ln_kernel(x_ref, a_ref, w_ref, g_ref, b_ref, o_ref, obf_ref):
    y = DEEPNORM_ALPHA * x_ref[...] + _dot(a_ref[...], w_ref[...])
    out = _layer_norm(y, g_ref[...], b_ref[...])
    o_ref[...] = out
    obf_ref[...] = out.astype(BF16)


def _ffn_up_kernel(x_ref, xp_ref, wg_ref, wv_ref, cwg_ref, cwv_ref, cbg_ref, cbv_ref, o_ref, *, tm, seq):
    i = pl.program_id(1)
    first = (i * tm) % seq == 0
    x = x_ref[...]
    xp = xp_ref[...]
    row = lax.broadcasted_iota(jnp.int32, (tm, 1), 0)

    def conv(w_ref, cw_ref, cb_ref):
        h = _dot(x, w_ref[...])
        hp = jnp.where(first, 0.0, _dot(xp, w_ref[...]))
        hm1 = jnp.where(row == 0, hp[7:8, :], pltpu.roll(h, 1, 0))
        hm2 = jnp.where(row == 0, hp[6:7, :], jnp.where(row == 1, hp[7:8, :], pltpu.roll(h, 2, 0)))
        return cw_ref[0:1, :] * hm2 + cw_ref[1:2, :] * hm1 + cw_ref[2:3, :] * h + cb_ref[...]

    hg = conv(wg_ref, cwg_ref, cbg_ref)
    hv = conv(wv_ref, cwv_ref, cbv_ref)
    o_ref[...] = (hg * jax.nn.sigmoid(hg) * hv).astype(o_ref.dtype)


def _ple_kernel(x_ref, p_ref, wpg_ref, bpg_ref, wpe_ref, o_ref):
    gate = jax.nn.sigmoid(_dot(x_ref[...], wpg_ref[...]) + bpg_ref[...])
    o_ref[...] = gate * _dot(p_ref[...], wpe_ref[...])


def _down_ln_kernel(x_ref, a_ref, w_ref, ple_ref, g_ref, b_ref, o_ref, acc_ref):
    k = pl.program_id(1)

    @pl.when(k == 0)
    def _():
        acc_ref[...] = DEEPNORM_ALPHA * x_ref[...] + ple_ref[...]

    acc_ref[...] += _dot(a_ref[...], w_ref[...])

    @pl.when(k == pl.num_programs(1) - 1)
    def _():
        o_ref[...] = _layer_norm(acc_ref[...], g_ref[...], b_ref[...])


def _mla_attention(q, k, v, *, tq=1024, tk=1024, hp=2, flags=None):
    bsz, seq, _ = q.shape
    assert tq % tk == 0
    return pl.pallas_call(
        functools.partial(_mla_attn_kernel, tq=tq, tk=tk, hp=hp),
        grid=(bsz, B_HEADS // hp, seq // tq),
        in_specs=[pl.BlockSpec((1, tq, hp * QK_PAD), lambda b, h, i: (b, i, h)),
                  pl.BlockSpec((1, seq, hp * QK_PAD), lambda b, h, i: (b, 0, h)),
                  pl.BlockSpec((1, seq, hp * V_HEAD), lambda b, h, i: (b, 0, h))],
        out_specs=pl.BlockSpec((1, tq, hp * V_HEAD), lambda b, h, i: (b, i, h)),
        out_shape=jax.ShapeDtypeStruct((bsz, seq, B_HEADS * V_HEAD), BF16),
        scratch_shapes=[pltpu.VMEM((hp * tq, LANES), F32), pltpu.VMEM((hp * tq, LANES), F32),
                        pltpu.VMEM((hp * tq, V_HEAD), F32)],
        compiler_params=_cparams(("parallel", "parallel", "arbitrary"), flags),
        name="mla_attn",
    )(q, k, v)


def _dsa_attention(qa, qidx, w_idx, kext, v_a, k_idx, aug, *, topk, q_blk=128, chunk=1024, group=8, ones=False,
                   blind=20, flags=None):
    bsz, seq, _ = qa.shape
    zk = jnp.zeros_like(k_idx)
    k_even = jnp.concatenate([k_idx, zk], axis=-1)
    k_odd = jnp.concatenate([zk, k_idx], axis=-1)
    aug_rows = jnp.repeat(aug, q_blk, axis=0)
    v_in = jnp.concatenate([v_a, jnp.ones_like(v_a)], axis=-1) if ones else v_a
    vw = v_in.shape[-1]
    return pl.pallas_call(
        functools.partial(_dsa_kernel, q_blk=q_blk, topk=topk, chunk=chunk, group=group, ones=ones, blind=blind),
        grid=(bsz, seq // q_blk),
        in_specs=[pl.BlockSpec((1, q_blk, A_HEADS * A_HEAD_DIM), lambda b, i: (b, i, 0)),
                  pl.BlockSpec((1, q_blk, IDX_HEADS * IDX_DIM), lambda b, i: (b, i, 0)),
                  pl.BlockSpec((1, q_blk, IDX_HEADS), lambda b, i: (b, i, 0)),
                  pl.BlockSpec((1, seq, 2 * A_HEAD_DIM), lambda b, i: (b, 0, 0)),
                  pl.BlockSpec((1, seq, vw), lambda b, i: (b, 0, 0)),
                  pl.BlockSpec((1, seq, LANES), lambda b, i: (b, 0, 0)),
                  pl.BlockSpec((1, seq, LANES), lambda b, i: (b, 0, 0)),
                  pl.BlockSpec((A_HEADS * q_blk, LANES), lambda b, i: (0, 0))],
        out_specs=pl.BlockSpec((1, q_blk, A_HEADS * A_HEAD_DIM), lambda b, i: (b, i, 0)),
        out_shape=jax.ShapeDtypeStruct((bsz, seq, A_HEADS * A_HEAD_DIM), BF16),
        scratch_shapes=[pltpu.VMEM((q_blk, seq), jnp.int32),
                        pltpu.VMEM((A_HEADS * q_blk, 2 * A_HEAD_DIM), BF16),
                        pltpu.VMEM((IDX_HEADS // 2 * q_blk, LANES), BF16),
                        pltpu.VMEM((IDX_HEADS, q_blk, LANES), F32),
                        pltpu.VMEM((A_HEADS * q_blk, LANES), F32),
                        pltpu.VMEM((8, LANES) if ones else (A_HEADS * q_blk, LANES), F32),
                        pltpu.VMEM((A_HEADS * q_blk, vw), F32)],
        compiler_params=_cparams(("parallel", "arbitrary"), flags),
        name="dsa_attn",
    )(qa, qidx, w_idx, kext, v_in, k_even, k_odd, aug_rows)


def _ffn_up(x1b, w_up_b, cw, cb, seq, *, tm=1024, tn=512):
    m, d = x1b.shape
    nft = D_FF // tn
    return pl.pallas_call(
        functools.partial(_ffn_up_kernel, tm=tm, seq=seq),
        grid=(nft, m // tm),
        in_specs=[pl.BlockSpec((tm, d), lambda j, i: (i, 0)),
                  pl.BlockSpec((8, d), lambda j, i: (jnp.maximum(i * (tm // 8) - 1, 0), 0)),
                  pl.BlockSpec((d, tn), lambda j, i: (0, j)),
                  pl.BlockSpec((d, tn), lambda j, i: (0, j + nft)),
                  pl.BlockSpec((CONV_W, tn), lambda j, i: (0, j)),
                  pl.BlockSpec((CONV_W, tn), lambda j, i: (0, j + nft)),
                  pl.BlockSpec((1, tn), lambda j, i: (0, j)),
                  pl.BlockSpec((1, tn), lambda j, i: (0, j + nft))],
        out_specs=pl.BlockSpec((tm, tn), lambda j, i: (i, j)),
        out_shape=jax.ShapeDtypeStruct((m, D_FF), BF16),
        compiler_params=_cparams(("parallel", "parallel")),
        name="ffn_up_conv",
    )(x1b, x1b, w_up_b, w_up_b, cw, cw, cb, cb)


def _ffn_down(x1, act, w_down_b, ple, g, b, *, tm=512, tk=1408):
    m, d = x1.shape
    return pl.pallas_call(
        _down_ln_kernel,
        grid=(m // tm, D_FF // tk),
        in_specs=[pl.BlockSpec((tm, d), lambda i, k: (i, 0)),
                  pl.BlockSpec((tm, tk), lambda i, k: (i, k)),
                  pl.BlockSpec((tk, d), lambda i, k: (k, 0)),
                  pl.BlockSpec((tm, d), lambda i, k: (i, 0)),
                  pl.BlockSpec((1, d), lambda i, k: (0, 0)),
                  pl.BlockSpec((1, d), lambda i, k: (0, 0))],
        out_specs=pl.BlockSpec((tm, d), lambda i, k: (i, 0)),
        out_shape=jax.ShapeDtypeStruct((m, d), F32),
        scratch_shapes=[pltpu.VMEM((tm, d), F32)],
        compiler_params=_cparams(("parallel", "arbitrary")),
        name="ffn_down_ln2",
    )(x1, act, w_down_b, ple, g, b)


def _split_bf16(x, parts):
    out = []
    rem = x
    for _ in range(parts):
        piece = rem.astype(BF16)
        out.append(piece)
        rem = rem - piece.astype(F32)
    return out


def kernel(x, p, w_in, q_a_norm, kv_a_norm, w_uq, w_uk, w_uv, w_branch_a, w_branch_b, w_o,
           ln1_g, ln1_b, w_up, conv_w, conv_b, w_down, w_pg, b_pg, w_pe, ln2_g, ln2_b):
    bsz, seq, d = x.shape
    m = bsz * seq
    topk = min(INDEX_TOPK_MAX, seq // 4)
    row = lambda a: a.reshape(1, -1)

    wi = w_in[0]
    o_qa, o_ka, o_va = 0, 2048, 2176
    o_qi, o_ki, o_wi, o_cq, o_ckv, o_kpe, o_gate = 2304, 3328, 3392, 3408, 3920, 4432, 4496
    w_qa = wi[:, o_qa:o_ka].astype(BF16)
    w_kv = wi[:, o_ka:o_qi].astype(BF16)
    w_qi = wi[:, o_qi:o_ki].astype(BF16)
    w_misc = jnp.pad(wi[:, o_ki:o_cq], ((0, 0), (0, LANES - (o_cq - o_ki)))).astype(BF16)
    w_lat = jnp.pad(wi[:, o_cq:o_gate], ((0, 0), (0, LANES - QK_ROPE))).astype(BF16)
    w_gate = wi[:, o_gate:].astype(BF16)

    wq = jnp.pad(w_uq[0], ((0, 0), (0, 0), (0, QK_PAD - QK_NOPE - QK_ROPE)))
    wq = wq.reshape(Q_LORA, B_HEADS * QK_PAD).astype(BF16)
    wk = w_uk[0].reshape(KV_LORA, B_HEADS * QK_NOPE).astype(BF16)
    wv = w_uv[0].reshape(KV_LORA, B_HEADS * V_HEAD).astype(BF16)

    pos = jnp.arange(seq, dtype=F32)
    inv_freq = jnp.power(ROPE_THETA, -jnp.arange(0, QK_ROPE, 2, dtype=F32) / QK_ROPE)
    ang = pos[:, None] * inv_freq[None, :]
    cos, sin = jnp.cos(ang), jnp.sin(ang)
    zpad = jnp.zeros((seq, LANES - QK_ROPE), F32)
    cos2 = jnp.concatenate([cos, cos, zpad], axis=1)
    sin2 = jnp.concatenate([-sin, sin, zpad], axis=1)

    slopes = LOG2E * jnp.exp2(-8.0 * jnp.arange(1, A_HEADS + 1, dtype=F32) / A_HEADS)
    sl = _split_bf16(slopes, 3)
    aug = jnp.stack([s_ * 64.0 for s_ in sl] + sl, axis=1).astype(BF16)
    aug = jnp.pad(aug, ((0, 0), (0, LANES - 6)))
    s_int = jnp.arange(seq, dtype=jnp.int32)
    s_hi = (s_int // 64).astype(BF16)
    s_lo = (s_int % 64).astype(BF16)
    kaug = jnp.stack([s_hi, s_hi, s_hi, s_lo, s_lo, s_lo], axis=1)
    kaug = jnp.pad(kaug, ((0, 0), (0, LANES - 6)))

    x2 = x.reshape(m, d)
    xb = x2.astype(BF16)

    qa = _matmul(xb, w_qa, BF16, scale=LOG2E * A_HEAD_DIM ** -0.5)
    kv = _matmul(xb, w_kv, BF16)
    qidx = _matmul(xb, w_qi, BF16)
    misc = _matmul(xb, w_misc, F32)
    lat = _matmul(xb, w_lat, F32, tn=1152)
    gates = _matmul(xb, w_gate, F32, act="sigmoid")

    tm = 256
    nst = seq // tm
    q_mla = pl.pallas_call(
        functools.partial(_mla_q_kernel, scale=LOG2E * (QK_NOPE + QK_ROPE) ** -0.5),
        grid=(m // tm,),
        in_specs=[pl.BlockSpec((tm, Q_LORA), lambda i: (i, 0)),
                  pl.BlockSpec((1, Q_LORA), lambda i: (0, 0)),
                  pl.BlockSpec((Q_LORA, B_HEADS * QK_PAD), lambda i: (0, 0)),
                  pl.BlockSpec((tm, LANES), lambda i: (i % nst, 0)),
                  pl.BlockSpec((tm, LANES), lambda i: (i % nst, 0))],
        out_specs=pl.BlockSpec((tm, B_HEADS * QK_PAD), lambda i: (i, 0)),
        out_shape=jax.ShapeDtypeStruct((m, B_HEADS * QK_PAD), BF16),
        compiler_params=_cparams(("parallel",)),
        name="mla_q_proj",
    )(lat, row(q_a_norm[0]), wq, cos2, sin2)

    k_mla, v_mla = pl.pallas_call(
        _mla_kv_kernel,
        grid=(m // tm,),
        in_specs=[pl.BlockSpec((tm, KV_LORA), lambda i: (i, 1)),
                  pl.BlockSpec((tm, LANES), lambda i: (i, (Q_LORA + KV_LORA) // LANES)),
                  pl.BlockSpec((1, KV_LORA), lambda i: (0, 0)),
                  pl.BlockSpec((KV_LORA, B_HEADS * QK_NOPE), lambda i: (0, 0)),
                  pl.BlockSpec((KV_LORA, B_HEADS * V_HEAD), lambda i: (0, 0)),
                  pl.BlockSpec((tm, LANES), lambda i: (i % nst, 0)),
                  pl.BlockSpec((tm, LANES), lambda i: (i % nst, 0))],
        out_specs=[pl.BlockSpec((tm, B_HEADS * QK_PAD), lambda i: (i, 0)),
                   pl.BlockSpec((tm, B_HEADS * V_HEAD), lambda i: (i, 0))],
        out_shape=[jax.ShapeDtypeStruct((m, B_HEADS * QK_PAD), BF16),
                   jax.ShapeDtypeStruct((m, B_HEADS * V_HEAD), BF16)],
        compiler_params=_cparams(("parallel",)),
        name="mla_kv_proj",
    )(lat, lat, row(kv_a_norm[0]), wk, wv, cos2, sin2)

    attn_b = _mla_attention(q_mla.reshape(bsz, seq, -1), k_mla.reshape(bsz, seq, -1), v_mla.reshape(bsz, seq, -1))

    k_a = kv[:, :A_HEAD_DIM].reshape(bsz, seq, A_HEAD_DIM)
    v_a = kv[:, A_HEAD_DIM:].reshape(bsz, seq, A_HEAD_DIM)
    kext = jnp.concatenate([k_a, jnp.broadcast_to(kaug[None], (bsz, seq, LANES))], axis=-1)
    k_idx = misc[:, :IDX_DIM].astype(BF16).reshape(bsz, seq, IDX_DIM)
    w_idx = misc[:, IDX_DIM:IDX_DIM + IDX_HEADS].reshape(bsz, seq, IDX_HEADS)
    attn_a = _dsa_attention(qa.reshape(bsz, seq, -1), qidx.reshape(bsz, seq, -1), w_idx, kext, v_a, k_idx, aug,
                            topk=topk)

    tmm, tnm = 1024, 512
    merged = pl.pallas_call(
        _merge_kernel,
        grid=(d // tnm, m // tmm),
        in_specs=[pl.BlockSpec((tmm, d), lambda j, i: (i, 0)),
                  pl.BlockSpec((tmm, d), lambda j, i: (i, 0)),
                  pl.BlockSpec((d, tnm), lambda j, i: (0, j)),
                  pl.BlockSpec((d, tnm), lambda j, i: (0, j)),
                  pl.BlockSpec((tmm, tnm), lambda j, i: (i, j)),
                  pl.BlockSpec((tmm, tnm), lambda j, i: (i, j + d // tnm))],
        out_specs=pl.BlockSpec((tmm, tnm), lambda j, i: (i, j)),
        out_shape=jax.ShapeDtypeStruct((m, d), BF16),
        compiler_params=_cparams(("parallel", "parallel")),
        name="branch_merge",
    )(attn_a.reshape(m, d), attn_b.reshape(m, d), w_branch_a[0].astype(BF16), w_branch_b[0].astype(BF16),
      gates, gates)

    tml = 256
    x1, x1b = pl.pallas_call(
        _proj_ln_kernel,
        grid=(m // tml,),
        in_specs=[pl.BlockSpec((tml, d), lambda i: (i, 0)),
                  pl.BlockSpec((tml, d), lambda i: (i, 0)),
                  pl.BlockSpec((d, d), lambda i: (0, 0)),
                  pl.BlockSpec((1, d), lambda i: (0, 0)),
                  pl.BlockSpec((1, d), lambda i: (0, 0))],
        out_specs=[pl.BlockSpec((tml, d), lambda i: (i, 0)), pl.BlockSpec((tml, d), lambda i: (i, 0))],
        out_shape=[jax.ShapeDtypeStruct((m, d), F32), jax.ShapeDtypeStruct((m, d), BF16)],
        compiler_params=_cparams(("parallel",)),
        name="oproj_ln1",
    )(x2, merged, w_o[0].astype(BF16), row(ln1_g[0]), row(ln1_b[0]))

    act = _ffn_up(x1b, w_up[0].astype(BF16), conv_w[0], row(conv_b[0]), seq)

    tnp = 1024
    ple = pl.pallas_call(
        _ple_kernel,
        grid=(d // tnp, m // tmm),
        in_specs=[pl.BlockSpec((tmm, d), lambda j, i: (i, 0)),
                  pl.BlockSpec((tmm, PLE_DIM), lambda j, i: (i, 0)),
                  pl.BlockSpec((d, tnp), lambda j, i: (0, j)),
                  pl.BlockSpec((1, tnp), lambda j, i: (0, j)),
                  pl.BlockSpec((PLE_DIM, tnp), lambda j, i: (0, j))],
        out_specs=pl.BlockSpec((tmm, tnp), lambda j, i: (i, j)),
        out_shape=jax.ShapeDtypeStruct((m, d), F32),
        compiler_params=_cparams(("parallel", "parallel")),
        name="ple_gate",
    )(x1b, p[0].reshape(m, PLE_DIM).astype(BF16), w_pg[0].astype(BF16), row(b_pg[0]), w_pe[0].astype(BF16))

    out = _ffn_down(x1, act, w_down[0].astype(BF16), ple, row(ln2_g[0]), row(ln2_b[0]))
    return out.reshape(bsz, seq, d)
```

```python
import functools

import jax
import jax.numpy as jnp
import numpy as np
from jax import lax
from jax.experimental import pallas as pl
from jax.experimental.pallas import tpu as pltpu

F32 = jnp.float32
BF16 = jnp.bfloat16

D_MODEL = 2048
A_HEADS = 16
A_HEAD_DIM = 128
IDX_HEADS = 16
IDX_DIM = 64
INDEX_TOPK_MAX = 256
B_HEADS = 16
Q_LORA = 512
KV_LORA = 512
QK_NOPE = 128
QK_ROPE = 64
V_HEAD = 128
ROPE_THETA = 10000.0
D_FF = 5632
CONV_W = 3
PLE_DIM = 256
LN_EPS = 1e-5
RMS_EPS = 1e-6
NEG = -1e30
DEEPNORM_ALPHA = 2.0 ** 0.25
LOG2E = 1.4426950408889634

LANES = 128
QK_PAD = 256
VE = 2 * LANES
VMEM_LIMIT = 56 * 1024 * 1024
INT_MIN = -2 ** 31
INT_MAX = 2 ** 31 - 1


def _cparams(sem, flags=None):
    return pltpu.CompilerParams(dimension_semantics=sem, vmem_limit_bytes=VMEM_LIMIT, flags=flags)


def _dot(a, b):
    return jnp.dot(a, b, preferred_element_type=F32)


def _dot_nt(a, b):
    return lax.dot_general(a, b, (((1,), (1,)), ((), ())), preferred_element_type=F32)


def _layer_norm(y, g, b):
    mu = jnp.mean(y, axis=-1, keepdims=True)
    d = y - mu
    var = jnp.mean(d * d, axis=-1, keepdims=True)
    return d * lax.rsqrt(var + LN_EPS) * g + b


def _rms_norm(c, g):
    return c * lax.rsqrt(jnp.mean(c * c, axis=-1, keepdims=True) + RMS_EPS) * g


def _rope_lanes(pe, cos2, sin2):
    lane = lax.broadcasted_iota(jnp.int32, pe.shape, 1)
    swapped = jnp.where(lane < QK_ROPE // 2, pltpu.roll(pe, LANES - QK_ROPE // 2, 1),
                        pltpu.roll(pe, QK_ROPE // 2, 1))
    return pe * cos2 + swapped * sin2


def _mm_kernel(x_ref, w_ref, o_ref, *, scale, act):
    y = _dot(x_ref[...], w_ref[...])
    if scale != 1.0:
        y = y * scale
    if act == "sigmoid":
        y = jax.nn.sigmoid(y)
    o_ref[...] = y.astype(o_ref.dtype)


def _matmul(x, w, out_dtype, *, scale=1.0, act=None, tm=1024, tn=1024):
    m, k = x.shape
    n = w.shape[1]
    tn = min(tn, n)
    assert m % tm == 0 and n % tn == 0
    return pl.pallas_call(
        functools.partial(_mm_kernel, scale=scale, act=act),
        grid=(n // tn, m // tm),
        in_specs=[pl.BlockSpec((tm, k), lambda j, i: (i, 0)),
                  pl.BlockSpec((k, tn), lambda j, i: (0, j))],
        out_specs=pl.BlockSpec((tm, tn), lambda j, i: (i, j)),
        out_shape=jax.ShapeDtypeStruct((m, n), out_dtype),
        compiler_params=_cparams(("parallel", "parallel")),
        name="proj_mm",
    )(x, w)


def _mla_q_kernel(c_ref, g_ref, w_ref, cos_ref, sin_ref, o_ref, *, scale):
    y = _rms_norm(c_ref[...], g_ref[...]).astype(BF16)
    q = _dot(y, w_ref[...])
    cos2 = cos_ref[...]
    sin2 = sin_ref[...]
    for h in range(B_HEADS):
        base = h * QK_PAD
        o_ref[:, base:base + QK_NOPE] = (q[:, base:base + QK_NOPE] * scale).astype(BF16)
        pe = q[:, base + QK_NOPE:base + QK_PAD]
        o_ref[:, base + QK_NOPE:base + QK_PAD] = (_rope_lanes(pe, cos2, sin2) * scale).astype(BF16)


def _mla_kv_kernel(c_ref, kpe_ref, g_ref, wk_ref, wv_ref, cos_ref, sin_ref, k_ref, v_ref):
    y = _rms_norm(c_ref[...], g_ref[...]).astype(BF16)
    kn = _dot(y, wk_ref[...])
    v_ref[...] = _dot(y, wv_ref[...]).astype(BF16)
    kpe = _rope_lanes(kpe_ref[...], cos_ref[...], sin_ref[...]).astype(BF16)
    for h in range(B_HEADS):
        base = h * QK_PAD
        k_ref[:, base:base + QK_NOPE] = kn[:, h * QK_NOPE:(h + 1) * QK_NOPE].astype(BF16)
        k_ref[:, base + QK_NOPE:base + QK_PAD] = kpe


def _flash_update(s, v_chunk, m_ref, l_ref, acc_ref, rows):
    nblk = s.shape[1] // LANES
    m_prev = m_ref[rows, :]
    m_new = jnp.maximum(m_prev, jnp.max(s, axis=-1, keepdims=True))
    alpha = jnp.exp2(m_prev - m_new)
    ps = [jnp.exp2(s[:, j * LANES:(j + 1) * LANES] - m_new) for j in range(nblk)]
    p = jnp.concatenate([x.astype(BF16) for x in ps], axis=1)
    if l_ref is None:
        acc_ref[rows, :] = jnp.concatenate([alpha, alpha], axis=1) * acc_ref[rows, :] + _dot(p, v_chunk)
    else:
        l_ref[rows, :] = alpha * l_ref[rows, :] + functools.reduce(lambda a, b: a + b, ps)
        acc_ref[rows, :] = alpha * acc_ref[rows, :] + _dot(p, v_chunk)
    m_ref[rows, :] = m_new


def _flash_finish(l_ref, acc_ref, rows):
    if l_ref is None:
        return acc_ref[rows, 0:LANES] / acc_ref[rows, LANES:VE]
    return acc_ref[rows, :] / jnp.sum(l_ref[rows, :], axis=-1, keepdims=True)


def _mla_attn_kernel(q_ref, k_ref, v_ref, o_ref, m_ref, l_ref, acc_ref, *, tq, tk, td, hp):
    qi = pl.program_id(2)
    m_ref[...] = jnp.full(m_ref.shape, NEG, F32)
    l_ref[...] = jnp.zeros(l_ref.shape, F32)
    acc_ref[...] = jnp.zeros(acc_ref.shape, F32)

    def tile(off, width, r0, diag):
        for h in range(hp):
            q = q_ref[0, r0:tq, h * QK_PAD:(h + 1) * QK_PAD]
            s = _dot_nt(q, k_ref[0, pl.ds(off, width), h * QK_PAD:(h + 1) * QK_PAD])
            if diag is not None:
                row = lax.broadcasted_iota(jnp.int32, s.shape, 0) + r0
                col = lax.broadcasted_iota(jnp.int32, s.shape, 1) + diag
                s = jnp.where(col <= row, s, NEG)
            _flash_update(s, v_ref[0, pl.ds(off, width), h * V_HEAD:(h + 1) * V_HEAD],
                          m_ref, l_ref, acc_ref, slice(h * tq + r0, (h + 1) * tq))

    def body(c, carry):
        tile(pl.multiple_of(c * tk, tk), tk, 0, None)
        return carry

    lax.fori_loop(0, qi * (tq // tk), body, 0)
    for d in range(tq // td):
        tile(pl.multiple_of(qi * tq + d * td, td), td, d * td, d * td)
    for h in range(hp):
        o_ref[0, :, h * V_HEAD:(h + 1) * V_HEAD] = _flash_finish(
            l_ref, acc_ref, slice(h * tq, (h + 1) * tq)).astype(o_ref.dtype)


def _monotone_key(x):
    bits = lax.bitcast_convert_type(x, jnp.int32)
    return bits ^ ((bits >> 31) & INT_MAX)


def _dsa_kernel(qa_ref, qi_ref, w_ref, kext_ref, v_ref, ke_ref, ko_ref, aug_ref, o_ref,
                key_ref, lhs_ref, qs_ref, wb_ref, m_ref, l_ref, acc_ref,
                *, q_blk, topk, chunk, group, ones, blind):
    if ones:
        l_ref = None
    qb = pl.program_id(1)
    n_chunks = (qb * q_blk + q_blk + chunk - 1) // chunk
    rep = chunk // LANES
    n_pairs = IDX_HEADS // 2

    for h in range(A_HEADS):
        lhs_ref[h * q_blk:(h + 1) * q_blk, 0:A_HEAD_DIM] = qa_ref[0, :, h * A_HEAD_DIM:(h + 1) * A_HEAD_DIM]
    lhs_ref[:, A_HEAD_DIM:2 * A_HEAD_DIM] = aug_ref[...]
    for j in range(n_pairs):
        qs_ref[j * q_blk:(j + 1) * q_blk, :] = qi_ref[0, :, j * LANES:(j + 1) * LANES]
    for h in range(IDX_HEADS):
        wb_ref[h] = jnp.broadcast_to(w_ref[0, :, h:h + 1], (q_blk, LANES))

    t_col = qb * q_blk + lax.broadcasted_iota(jnp.int32, (q_blk, 1), 0)

    def score_chunk(c, carry):
        off = pl.multiple_of(c * chunk, chunk)
        qs = qs_ref[...]
        se = _dot_nt(qs, ke_ref[0, pl.ds(off, chunk), :])
        so = _dot_nt(qs, ko_ref[0, pl.ds(off, chunk), :])
        acc = jnp.zeros((q_blk, chunk), F32)
        for j in range(n_pairs):
            we = jnp.concatenate([wb_ref[2 * j]] * rep, axis=1)
            wo = jnp.concatenate([wb_ref[2 * j + 1]] * rep, axis=1)
            acc = acc + we * jnp.maximum(se[j * q_blk:(j + 1) * q_blk], 0.0)
            acc = acc + wo * jnp.maximum(so[j * q_blk:(j + 1) * q_blk], 0.0)
        s_pos = off + lax.broadcasted_iota(jnp.int32, (q_blk, chunk), 1)
        key_ref[:, pl.ds(off, chunk)] = jnp.where(s_pos <= t_col, _monotone_key(acc), INT_MIN)
        return carry

    lax.fori_loop(0, n_chunks, score_chunk, 0)

    kf = float(topk)

    def count_ge(th):
        thb = jnp.broadcast_to(th, (q_blk, LANES))

        def body(c, cnt):
            off = pl.multiple_of(c * chunk, chunk)
            blk = key_ref[:, pl.ds(off, chunk)]
            for j in range(rep):
                cnt = cnt + jnp.where(blk[:, j * LANES:(j + 1) * LANES] >= thb, 1.0, 0.0)
            return cnt

        cnt = lax.fori_loop(0, n_chunks, body, jnp.zeros((q_blk, LANES), F32))
        return jnp.sum(cnt, axis=-1, keepdims=True)

    n_causal = (t_col + 1).astype(F32)
    few = n_causal <= kf
    lo0 = jnp.full((q_blk, 1), INT_MIN + 1, jnp.int32)
    hi0 = jnp.where(few, INT_MIN + 2, INT_MAX).astype(jnp.int32)
    cnt_lo0 = n_causal
    cnt_hi0 = jnp.zeros((q_blk, 1), F32)

    def active(lo, hi, cnt_lo):
        return jnp.logical_and(lo + 1 < hi, cnt_lo != kf)

    def bis_step(st):
        lo, hi, cnt_lo, cnt_hi = st
        act = active(lo, hi, cnt_lo)
        mid = (lo & hi) + ((lo ^ hi) >> 1)
        c = count_ge(mid)
        ge = jnp.logical_and(act, c >= kf)
        lt = jnp.logical_and(act, c < kf)
        return (jnp.where(ge, mid, lo), jnp.where(lt, mid, hi), jnp.where(ge, c, cnt_lo), jnp.where(lt, c, cnt_hi))

    def bis_cond(st):
        it, lo, hi, cnt_lo, cnt_hi = st
        n_act = jnp.max(jnp.where(active(lo, hi, cnt_lo), 1, 0))
        return jnp.logical_and(it < 32 - blind + 2, n_act > 0)

    st = lax.fori_loop(0, blind, lambda _, s_: bis_step(s_), (lo0, hi0, cnt_lo0, cnt_hi0))
    _, v_key, _, cnt_lo, cnt_hi = lax.while_loop(
        bis_cond, lambda s_: (s_[0] + 1,) + bis_step(s_[1:]), (jnp.int32(0),) + st)

    tied = jnp.logical_and(cnt_lo > kf, jnp.logical_not(few))
    need = kf - cnt_hi
    n_tied = jnp.max(jnp.where(tied, 1, 0))

    def count_tie_le(jj):
        vb = jnp.broadcast_to(v_key, (q_blk, LANES))
        jb = jnp.broadcast_to(jj, (q_blk, LANES))
        lane = lax.broadcasted_iota(jnp.int32, (q_blk, LANES), 1)

        def body(c, cnt):
            off = pl.multiple_of(c * chunk, chunk)
            blk = key_ref[:, pl.ds(off, chunk)]
            for j in range(rep):
                in_range = jnp.where(off + j * LANES + lane <= jb, 1.0, 0.0)
                cnt = cnt + jnp.where(blk[:, j * LANES:(j + 1) * LANES] == vb, in_range, 0.0)
            return cnt

        cnt = lax.fori_loop(0, n_chunks, body, jnp.zeros((q_blk, LANES), F32))
        return jnp.sum(cnt, axis=-1, keepdims=True)

    def tie_search(_):
        def body(_, st):
            lo, hi = st
            mid = (lo + hi) >> 1
            ok = count_tie_le(mid) >= need
            upd = lo + 1 < hi
            return (jnp.where(jnp.logical_and(upd, jnp.logical_not(ok)), mid, lo),
                    jnp.where(jnp.logical_and(upd, ok), mid, hi))

        lo = jnp.full((q_blk, 1), -1, jnp.int32)
        hi = jnp.broadcast_to(t_col, (q_blk, 1))
        _, hi = lax.fori_loop(0, 14, body, (lo, hi))
        return jnp.where(tied, hi, INT_MAX)

    j_max = lax.cond(n_tied > 0, tie_search, lambda _: jnp.full((q_blk, 1), INT_MAX, jnp.int32), 0)

    m_ref[...] = jnp.full(m_ref.shape, NEG, F32)
    acc_ref[...] = jnp.zeros(acc_ref.shape, F32)
    if l_ref is not None:
        l_ref[...] = jnp.zeros(l_ref.shape, F32)
    rows = group * q_blk
    n_groups = A_HEADS // group

    def attn_chunk(c, carry):
        off = pl.multiple_of(c * chunk, chunk)
        keys = key_ref[:, pl.ds(off, chunk)]
        s_pos = off + lax.broadcasted_iota(jnp.int32, (q_blk, chunk), 1)
        sel = jnp.where(keys > v_key, 1, jnp.where(keys == v_key, jnp.where(s_pos <= j_max, 1, 0), 0))
        bias = jnp.where(sel > 0, 0.0, NEG)
        kc = kext_ref[0, pl.ds(off, chunk), :]
        vc = v_ref[0, pl.ds(off, chunk), :]
        for g in range(n_groups):
            r0 = g * rows
            s = _dot_nt(lhs_ref[r0:r0 + rows, :], kc)
            s = (s.reshape(group, q_blk, chunk) + bias[None]).reshape(rows, chunk)
            _flash_update(s, vc, m_ref, l_ref, acc_ref, slice(r0, r0 + rows))
        return carry

    lax.fori_loop(0, n_chunks, attn_chunk, 0)
    for h in range(A_HEADS):
        r0 = h * q_blk
        o_ref[0, :, h * A_HEAD_DIM:(h + 1) * A_HEAD_DIM] = _flash_finish(
            l_ref, acc_ref, slice(r0, r0 + q_blk)).astype(o_ref.dtype)


def _merge_kernel(a_ref, b_ref, wa_ref, wb_ref, g0_ref, g1_ref, o_ref):
    ya = _dot(a_ref[...], wa_ref[...])
    yb = _dot(b_ref[...], wb_ref[...])
    o_ref[...] = (g0_ref[...] * ya + g1_ref[...] * yb).astype(o_ref.dtype)


def _proj_ln_kernel(x_ref, a_ref, w_ref, g_ref, b_ref, o_ref, obf_ref):
    y = DEEPNORM_ALPHA * x_ref[...] + _dot(a_ref[...], w_ref[...])
    out = _layer_norm(y, g_ref[...], b_ref[...])
    o_ref[...] = out
    obf_ref[...] = out.astype(BF16)


def _ffn_up_kernel(x_ref, xp_ref, wg_ref, wv_ref, cwg_ref, cwv_ref, cbg_ref, cbv_ref, o_ref, *, tm, seq):
    i = pl.program_id(1)
    first = (i * tm) % seq == 0
    x = x_ref[...]
    xp = xp_ref[...]
    row = lax.broadcasted_iota(jnp.int32, (tm, 1), 0)

    def conv(w_ref, cw_ref, cb_ref):
        h = _dot(x, w_ref[...])
        hp = jnp.where(first, 0.0, _dot(xp, w_ref[...]))
        hm1 = jnp.where(row == 0, hp[7:8, :], pltpu.roll(h, 1, 0))
        hm2 = jnp.where(row == 0, hp[6:7, :], jnp.where(row == 1, hp[7:8, :], pltpu.roll(h, 2, 0)))
        return cw_ref[0:1, :] * hm2 + cw_ref[1:2, :] * hm1 + cw_ref[2:3, :] * h + cb_ref[...]

    hg = conv(wg_ref, cwg_ref, cbg_ref)
    hv = conv(wv_ref, cwv_ref, cbv_ref)
    o_ref[...] = (hg * jax.nn.sigmoid(hg) * hv).astype(o_ref.dtype)


def _ple_kernel(x_ref, p_ref, wpg_ref, bpg_ref, wpe_ref, o_ref):
    gate = jax.nn.sigmoid(_dot(x_ref[...], wpg_ref[...]) + bpg_ref[...])
    o_ref[...] = gate * _dot(p_ref[...], wpe_ref[...])


def _down_ln_kernel(x_ref, a_ref, w_ref, ple_ref, g_ref, b_ref, o_ref, acc_ref):
    k = pl.program_id(1)

    @pl.when(k == 0)
    def _():
        acc_ref[...] = DEEPNORM_ALPHA * x_ref[...] + ple_ref[...]

    acc_ref[...] += _dot(a_ref[...], w_ref[...])

    @pl.when(k == pl.num_programs(1) - 1)
    def _():
        o_ref[...] = _layer_norm(acc_ref[...], g_ref[...], b_ref[...])


def _mla_attention(q, k, v, *, tq=1024, tk=1024, td=256, hp=2, flags=None):
    bsz, seq, _ = q.shape
    assert tq % tk == 0 and tq % td == 0
    return pl.pallas_call(
        functools.partial(_mla_attn_kernel, tq=tq, tk=tk, td=td, hp=hp),
        grid=(bsz, B_HEADS // hp, seq // tq),
        in_specs=[pl.BlockSpec((1, tq, hp * QK_PAD), lambda b, h, i: (b, i, h)),
                  pl.BlockSpec((1, seq, hp * QK_PAD), lambda b, h, i: (b, 0, h)),
                  pl.BlockSpec((1, seq, hp * V_HEAD), lambda b, h, i: (b, 0, h))],
        out_specs=pl.BlockSpec((1, tq, hp * V_HEAD), lambda b, h, i: (b, i, h)),
        out_shape=jax.ShapeDtypeStruct((bsz, seq, B_HEADS * V_HEAD), BF16),
        scratch_shapes=[pltpu.VMEM((hp * tq, LANES), F32), pltpu.VMEM((hp * tq, LANES), F32),
                        pltpu.VMEM((hp * tq, V_HEAD), F32)],
        compiler_params=_cparams(("parallel", "parallel", "arbitrary"), flags),
        name="mla_attn",
    )(q, k, v)


def _dsa_attention(qa, qidx, w_idx, kext, v_a, k_idx, aug, *, topk, q_blk=128, chunk=1024, group=8, ones=False,
                   blind=20, flags=None):
    bsz, seq, _ = qa.shape
    zk = jnp.zeros_like(k_idx)
    k_even = jnp.concatenate([k_idx, zk], axis=-1)
    k_odd = jnp.concatenate([zk, k_idx], axis=-1)
    aug_rows = jnp.repeat(aug, q_blk, axis=0)
    v_in = jnp.concatenate([v_a, jnp.ones_like(v_a)], axis=-1) if ones else v_a
    vw = v_in.shape[-1]
    return pl.pallas_call(
        functools.partial(_dsa_kernel, q_blk=q_blk, topk=topk, chunk=chunk, group=group, ones=ones, blind=blind),
        grid=(bsz, seq // q_blk),
        in_specs=[pl.BlockSpec((1, q_blk, A_HEADS * A_HEAD_DIM), lambda b, i: (b, i, 0)),
                  pl.BlockSpec((1, q_blk, IDX_HEADS * IDX_DIM), lambda b, i: (b, i, 0)),
                  pl.BlockSpec((1, q_blk, IDX_HEADS), lambda b, i: (b, i, 0)),
                  pl.BlockSpec((1, seq, 2 * A_HEAD_DIM), lambda b, i: (b, 0, 0)),
                  pl.BlockSpec((1, seq, vw), lambda b, i: (b, 0, 0)),
                  pl.BlockSpec((1, seq, LANES), lambda b, i: (b, 0, 0)),
                  pl.BlockSpec((1, seq, LANES), lambda b, i: (b, 0, 0)),
                  pl.BlockSpec((A_HEADS * q_blk, LANES), lambda b, i: (0, 0))],
        out_specs=pl.BlockSpec((1, q_blk, A_HEADS * A_HEAD_DIM), lambda b, i: (b, i, 0)),
        out_shape=jax.ShapeDtypeStruct((bsz, seq, A_HEADS * A_HEAD_DIM), BF16),
        scratch_shapes=[pltpu.VMEM((q_blk, seq), jnp.int32),
                        pltpu.VMEM((A_HEADS * q_blk, 2 * A_HEAD_DIM), BF16),
                        pltpu.VMEM((IDX_HEADS // 2 * q_blk, LANES), BF16),
                        pltpu.VMEM((IDX_HEADS, q_blk, LANES), F32),
                        pltpu.VMEM((A_HEADS * q_blk, LANES), F32),
                        pltpu.VMEM((8, LANES) if ones else (A_HEADS * q_blk, LANES), F32),
                        pltpu.VMEM((A_HEADS * q_blk, vw), F32)],
        compiler_params=_cparams(("parallel", "arbitrary"), flags),
        name="dsa_attn",
    )(qa, qidx, w_idx, kext, v_in, k_even, k_odd, aug_rows)


def _ffn_up(x1b, w_up_b, cw, cb, seq, *, tm=1024, tn=512):
    m, d = x1b.shape
    nft = D_FF // tn
    return pl.pallas_call(
        functools.partial(_ffn_up_kernel, tm=tm, seq=seq),
        grid=(nft, m // tm),
        in_specs=[pl.BlockSpec((tm, d), lambda j, i: (i, 0)),
                  pl.BlockSpec((8, d), lambda j, i: (jnp.maximum(i * (tm // 8) - 1, 0), 0)),
                  pl.BlockSpec((d, tn), lambda j, i: (0, j)),
                  pl.BlockSpec((d, tn), lambda j, i: (0, j + nft)),
                  pl.BlockSpec((CONV_W, tn), lambda j, i: (0, j)),
                  pl.BlockSpec((CONV_W, tn), lambda j, i: (0, j + nft)),
                  pl.BlockSpec((1, tn), lambda j, i: (0, j)),
                  pl.BlockSpec((1, tn), lambda j, i: (0, j + nft))],
        out_specs=pl.BlockSpec((tm, tn), lambda j, i: (i, j)),
        out_shape=jax.ShapeDtypeStruct((m, D_FF), BF16),
        compiler_params=_cparams(("parallel", "parallel")),
        name="ffn_up_conv",
    )(x1b, x1b, w_up_b, w_up_b, cw, cw, cb, cb)


def _ffn_down(x1, act, w_down_b, ple, g, b, *, tm=512, tk=1408):
    m, d = x1.shape
    return pl.pallas_call(
        _down_ln_kernel,
        grid=(m // tm, D_FF // tk),
        in_specs=[pl.BlockSpec((tm, d), lambda i, k: (i, 0)),
                  pl.BlockSpec((tm, tk), lambda i, k: (i, k)),
                  pl.BlockSpec((tk, d), lambda i, k: (k, 0)),
                  pl.BlockSpec((tm, d), lambda i, k: (i, 0)),
                  pl.BlockSpec((1, d), lambda i, k: (0, 0)),
                  pl.BlockSpec((1, d), lambda i, k: (0, 0))],
        out_specs=pl.BlockSpec((tm, d), lambda i, k: (i, 0)),
        out_shape=jax.ShapeDtypeStruct((m, d), F32),
        scratch_shapes=[pltpu.VMEM((tm, d), F32)],
        compiler_params=_cparams(("parallel", "arbitrary")),
        name="ffn_down_ln2",
    )(x1, act, w_down_b, ple, g, b)


def _split_bf16(x, parts):
    out = []
    rem = x
    for _ in range(parts):
        piece = rem.astype(BF16)
        out.append(piece)
        rem = rem - piece.astype(F32)
    return out


def kernel(x, p, w_in, q_a_norm, kv_a_norm, w_uq, w_uk, w_uv, w_branch_a, w_branch_b, w_o,
           ln1_g, ln1_b, w_up, conv_w, conv_b, w_down, w_pg, b_pg, w_pe, ln2_g, ln2_b):
    bsz, seq, d = x.shape
    m = bsz * seq
    topk = min(INDEX_TOPK_MAX, seq // 4)
    row = lambda a: a.reshape(1, -1)

    wi = w_in[0]
    o_qa, o_ka, o_va = 0, 2048, 2176
    o_qi, o_ki, o_wi, o_cq, o_ckv, o_kpe, o_gate = 2304, 3328, 3392, 3408, 3920, 4432, 4496
    w_qa = wi[:, o_qa:o_ka].astype(BF16)
    w_kv = wi[:, o_ka:o_qi].astype(BF16)
    w_qi = wi[:, o_qi:o_ki].astype(BF16)
    w_misc = jnp.pad(wi[:, o_ki:o_cq], ((0, 0), (0, LANES - (o_cq - o_ki)))).astype(BF16)
    w_lat = jnp.pad(wi[:, o_cq:o_gate], ((0, 0), (0, LANES - QK_ROPE))).astype(BF16)
    w_gate = wi[:, o_gate:].astype(BF16)

    wq = jnp.pad(w_uq[0], ((0, 0), (0, 0), (0, QK_PAD - QK_NOPE - QK_ROPE)))
    wq = wq.reshape(Q_LORA, B_HEADS * QK_PAD).astype(BF16)
    wk = w_uk[0].reshape(KV_LORA, B_HEADS * QK_NOPE).astype(BF16)
    wv = w_uv[0].reshape(KV_LORA, B_HEADS * V_HEAD).astype(BF16)

    pos = jnp.arange(seq, dtype=F32)
    inv_freq = jnp.power(ROPE_THETA, -jnp.arange(0, QK_ROPE, 2, dtype=F32) / QK_ROPE)
    ang = pos[:, None] * inv_freq[None, :]
    cos, sin = jnp.cos(ang), jnp.sin(ang)
    zpad = jnp.zeros((seq, LANES - QK_ROPE), F32)
    cos2 = jnp.concatenate([cos, cos, zpad], axis=1)
    sin2 = jnp.concatenate([-sin, sin, zpad], axis=1)

    slopes = LOG2E * jnp.exp2(-8.0 * jnp.arange(1, A_HEADS + 1, dtype=F32) / A_HEADS)
    sl = _split_bf16(slopes, 3)
    aug = jnp.stack([s_ * 64.0 for s_ in sl] + sl, axis=1).astype(BF16)
    aug = jnp.pad(aug, ((0, 0), (0, LANES - 6)))
    s_int = jnp.arange(seq, dtype=jnp.int32)
    s_hi = (s_int // 64).astype(BF16)
    s_lo = (s_int % 64).astype(BF16)
    kaug = jnp.stack([s_hi, s_hi, s_hi, s_lo, s_lo, s_lo], axis=1)
    kaug = jnp.pad(kaug, ((0, 0), (0, LANES - 6)))

    x2 = x.reshape(m, d)
    xb = x2.astype(BF16)

    qa = _matmul(xb, w_qa, BF16, scale=LOG2E * A_HEAD_DIM ** -0.5)
    kv = _matmul(xb, w_kv, BF16)
    qidx = _matmul(xb, w_qi, BF16)
    misc = _matmul(xb, w_misc, F32)
    lat = _matmul(xb, w_lat, F32, tn=1152)
    gates = _matmul(xb, w_gate, F32, act="sigmoid")

    tm = 512
    nst = seq // tm
    q_mla = pl.pallas_call(
        functools.partial(_mla_q_kernel, scale=LOG2E * (QK_NOPE + QK_ROPE) ** -0.5),
        grid=(m // tm,),
        in_specs=[pl.BlockSpec((tm, Q_LORA), lambda i: (i, 0)),
                  pl.BlockSpec((1, Q_LORA), lambda i: (0, 0)),
                  pl.BlockSpec((Q_LORA, B_HEADS * QK_PAD), lambda i: (0, 0)),
                  pl.BlockSpec((tm, LANES), lambda i: (i % nst, 0)),
                  pl.BlockSpec((tm, LANES), lambda i: (i % nst, 0))],
        out_specs=pl.BlockSpec((tm, B_HEADS * QK_PAD), lambda i: (i, 0)),
        out_shape=jax.ShapeDtypeStruct((m, B_HEADS * QK_PAD), BF16),
        compiler_params=_cparams(("parallel",)),
        name="mla_q_proj",
    )(lat, row(q_a_norm[0]), wq, cos2, sin2)

    k_mla, v_mla = pl.pallas_call(
        _mla_kv_kernel,
        grid=(m // tm,),
        in_specs=[pl.BlockSpec((tm, KV_LORA), lambda i: (i, 1)),
                  pl.BlockSpec((tm, LANES), lambda i: (i, (Q_LORA + KV_LORA) // LANES)),
                  pl.BlockSpec((1, KV_LORA), lambda i: (0, 0)),
                  pl.BlockSpec((KV_LORA, B_HEADS * QK_NOPE), lambda i: (0, 0)),
                  pl.BlockSpec((KV_LORA, B_HEADS * V_HEAD), lambda i: (0, 0)),
                  pl.BlockSpec((tm, LANES), lambda i: (i % nst, 0)),
                  pl.BlockSpec((tm, LANES), lambda i: (i % nst, 0))],
        out_specs=[pl.BlockSpec((tm, B_HEADS * QK_PAD), lambda i: (i, 0)),
                   pl.BlockSpec((tm, B_HEADS * V_HEAD), lambda i: (i, 0))],
        out_shape=[jax.ShapeDtypeStruct((m, B_HEADS * QK_PAD), BF16),
                   jax.ShapeDtypeStruct((m, B_HEADS * V_HEAD), BF16)],
        compiler_params=_cparams(("parallel",)),
        name="mla_kv_proj",
    )(lat, lat, row(kv_a_norm[0]), wk, wv, cos2, sin2)

    attn_b = _mla_attention(q_mla.reshape(bsz, seq, -1), k_mla.reshape(bsz, seq, -1), v_mla.reshape(bsz, seq, -1))

    k_a = kv[:, :A_HEAD_DIM].reshape(bsz, seq, A_HEAD_DIM)
    v_a = kv[:, A_HEAD_DIM:].reshape(bsz, seq, A_HEAD_DIM)
    kext = jnp.concatenate([k_a, jnp.broadcast_to(kaug[None], (bsz, seq, LANES))], axis=-1)
    k_idx = misc[:, :IDX_DIM].astype(BF16).reshape(bsz, seq, IDX_DIM)
    w_idx = misc[:, IDX_DIM:IDX_DIM + IDX_HEADS].reshape(bsz, seq, IDX_HEADS)
    attn_a = _dsa_attention(qa.reshape(bsz, seq, -1), qidx.reshape(bsz, seq, -1), w_idx, kext, v_a, k_idx, aug,
                            topk=topk)

    tmm, tnm = 1024, 512
    merged = pl.pallas_call(
        _merge_kernel,
        grid=(d // tnm, m // tmm),
        in_specs=[pl.BlockSpec((tmm, d), lambda j, i: (i, 0)),
                  pl.BlockSpec((tmm, d), lambda j, i: (i, 0)),
                  pl.BlockSpec((d, tnm), lambda j, i: (0, j)),
                  pl.BlockSpec((d, tnm), lambda j, i: (0, j)),
                  pl.BlockSpec((tmm, tnm), lambda j, i: (i, j)),
                  pl.BlockSpec((tmm, tnm), lambda j, i: (i, j + d // tnm))],
        out_specs=pl.BlockSpec((tmm, tnm), lambda j, i: (i, j)),
        out_shape=jax.ShapeDtypeStruct((m, d), BF16),
        compiler_params=_cparams(("parallel", "parallel")),
        name="branch_merge",
    )(attn_a.reshape(m, d), attn_b.reshape(m, d), w_branch_a[0].astype(BF16), w_branch_b[0].astype(BF16),
      gates, gates)

    tml = 512
    x1, x1b = pl.pallas_call(
        _proj_ln_kernel,
        grid=(m // tml,),
        in_specs=[pl.BlockSpec((tml, d), lambda i: (i, 0)),
                  pl.BlockSpec((tml, d), lambda i: (i, 0)),
                  pl.BlockSpec((d, d), lambda i: (0, 0)),
                  pl.BlockSpec((1, d), lambda i: (0, 0)),
                  pl.BlockSpec((1, d), lambda i: (0, 0))],
        out_specs=[pl.BlockSpec((tml, d), lambda i: (i, 0)), pl.BlockSpec((tml, d), lambda i: (i, 0))],
        out_shape=[jax.ShapeDtypeStruct((m, d), F32), jax.ShapeDtypeStruct((m, d), BF16)],
        compiler_params=_cparams(("parallel",)),
        name="oproj_ln1",
    )(x2, merged, w_o[0].astype(BF16), row(ln1_g[0]), row(ln1_b[0]))

    act = _ffn_up(x1b, w_up[0].astype(BF16), conv_w[0], row(conv_b[0]), seq)

    tnp = 1024
    ple = pl.pallas_call(
        _ple_kernel,
        grid=(d // tnp, m // tmm),
        in_specs=[pl.BlockSpec((tmm, d), lambda j, i: (i, 0)),
                  pl.BlockSpec((tmm, PLE_DIM), lambda j, i: (i, 0)),
                  pl.BlockSpec((d, tnp), lambda j, i: (0, j)),
                  pl.BlockSpec((1, tnp), lambda j, i: (0, j)),
                  pl.BlockSpec((PLE_DIM, tnp), lambda j, i: (0, j))],
        out_specs=pl.BlockSpec((tmm, tnp), lambda j, i: (i, j)),
        out_shape=jax.ShapeDtypeStruct((m, d), F32),
        compiler_params=_cparams(("parallel", "parallel")),
        name="ple_gate",
    )(x1b, p[0].reshape(m, PLE_DIM).astype(BF16), w_pg[0].astype(BF16), row(b_pg[0]), w_pe[0].astype(BF16))

    out = _ffn_down(x1, act, w_down[0].astype(BF16), ple, row(ln2_g[0]), row(ln2_b[0]))
    return out.reshape(bsz, seq, d)
```

```python
import functools

import jax
import jax.numpy as jnp
import numpy as np
from jax import lax
from jax.experimental import pallas as pl
from jax.experimental.pallas import tpu as pltpu

F32 = jnp.float32
BF16 = jnp.bfloat16

D_MODEL = 2048
A_HEADS = 16
A_HEAD_DIM = 128
IDX_HEADS = 16
IDX_DIM = 64
INDEX_TOPK_MAX = 256
B_HEADS = 16
Q_LORA = 512
KV_LORA = 512
QK_NOPE = 128
QK_ROPE = 64
V_HEAD = 128
ROPE_THETA = 10000.0
D_FF = 5632
CONV_W = 3
PLE_DIM = 256
LN_EPS = 1e-5
RMS_EPS = 1e-6
NEG = -1e30
DEEPNORM_ALPHA = 2.0 ** 0.25
LOG2E = 1.4426950408889634

LANES = 128
QK_PAD = 256
VE = 2 * LANES
VMEM_LIMIT = 56 * 1024 * 1024
INT_MIN = -2 ** 31
INT_MAX = 2 ** 31 - 1


def _cparams(sem, flags=None):
    return pltpu.CompilerParams(dimension_semantics=sem, vmem_limit_bytes=VMEM_LIMIT, flags=flags)


def _dot(a, b):
    return jnp.dot(a, b, preferred_element_type=F32)


def _dot_nt(a, b):
    return lax.dot_general(a, b, (((1,), (1,)), ((), ())), preferred_element_type=F32)


def _layer_norm(y, g, b):
    mu = jnp.mean(y, axis=-1, keepdims=True)
    d = y - mu
    var = jnp.mean(d * d, axis=-1, keepdims=True)
    return d * lax.rsqrt(var + LN_EPS) * g + b


def _rms_norm(c, g):
    return c * lax.rsqrt(jnp.mean(c * c, axis=-1, keepdims=True) + RMS_EPS) * g


def _rope_lanes(pe, cos2, sin2):
    lane = lax.broadcasted_iota(jnp.int32, pe.shape, 1)
    swapped = jnp.where(lane < QK_ROPE // 2, pltpu.roll(pe, LANES - QK_ROPE // 2, 1),
                        pltpu.roll(pe, QK_ROPE // 2, 1))
    return pe * cos2 + swapped * sin2


def _mm_kernel(x_ref, w_ref, o_ref, *, scale, act):
    y = _dot(x_ref[...], w_ref[...])
    if scale != 1.0:
        y = y * scale
    if act == "sigmoid":
        y = jax.nn.sigmoid(y)
    o_ref[...] = y.astype(o_ref.dtype)


def _matmul(x, w, out_dtype, *, scale=1.0, act=None, tm=1024, tn=1024):
    m, k = x.shape
    n = w.shape[1]
    tn = min(tn, n)
    assert m % tm == 0 and n % tn == 0
    return pl.pallas_call(
        functools.partial(_mm_kernel, scale=scale, act=act),
        grid=(n // tn, m // tm),
        in_specs=[pl.BlockSpec((tm, k), lambda j, i: (i, 0)),
                  pl.BlockSpec((k, tn), lambda j, i: (0, j))],
        out_specs=pl.BlockSpec((tm, tn), lambda j, i: (i, j)),
        out_shape=jax.ShapeDtypeStruct((m, n), out_dtype),
        compiler_params=_cparams(("parallel", "parallel")),
        name="proj_mm",
    )(x, w)


def _mla_q_kernel(c_ref, g_ref, w_ref, cos_ref, sin_ref, o_ref, *, scale):
    y = _rms_norm(c_ref[...], g_ref[...]).astype(BF16)
    q = _dot(y, w_ref[...])
    cos2 = cos_ref[...]
    sin2 = sin_ref[...]
    for h in range(B_HEADS):
        base = h * QK_PAD
        o_ref[:, base:base + QK_NOPE] = (q[:, base:base + QK_NOPE] * scale).astype(BF16)
        pe = q[:, base + QK_NOPE:base + QK_PAD]
        o_ref[:, base + QK_NOPE:base + QK_PAD] = (_rope_lanes(pe, cos2, sin2) * scale).astype(BF16)


def _mla_kv_kernel(c_ref, kpe_ref, g_ref, wk_ref, wv_ref, cos_ref, sin_ref, k_ref, v_ref):
    y = _rms_norm(c_ref[...], g_ref[...]).astype(BF16)
    kn = _dot(y, wk_ref[...])
    v_ref[...] = _dot(y, wv_ref[...]).astype(BF16)
    kpe = _rope_lanes(kpe_ref[...], cos_ref[...], sin_ref[...]).astype(BF16)
    for h in range(B_HEADS):
        base = h * QK_PAD
        k_ref[:, base:base + QK_NOPE] = kn[:, h * QK_NOPE:(h + 1) * QK_NOPE].astype(BF16)
        k_ref[:, base + QK_NOPE:base + QK_PAD] = kpe


def _flash_update(s, v_chunk, m_ref, l_ref, acc_ref, rows):
    nblk = s.shape[1] // LANES
    m_prev = m_ref[rows, :]
    m_new = jnp.maximum(m_prev, jnp.max(s, axis=-1, keepdims=True))
    alpha = jnp.exp2(m_prev - m_new)
    ps = [jnp.exp2(s[:, j * LANES:(j + 1) * LANES] - m_new) for j in range(nblk)]
    p = jnp.concatenate([x.astype(BF16) for x in ps], axis=1)
    if l_ref is None:
        acc_ref[rows, :] = jnp.concatenate([alpha, alpha], axis=1) * acc_ref[rows, :] + _dot(p, v_chunk)
    else:
        l_ref[rows, :] = alpha * l_ref[rows, :] + functools.reduce(lambda a, b: a + b, ps)
        acc_ref[rows, :] = alpha * acc_ref[rows, :] + _dot(p, v_chunk)
    m_ref[rows, :] = m_new


def _flash_finish(l_ref, acc_ref, rows):
    if l_ref is None:
        return acc_ref[rows, 0:LANES] / acc_ref[rows, LANES:VE]
    return acc_ref[rows, :] / jnp.sum(l_ref[rows, :], axis=-1, keepdims=True)


def _mla_attn_kernel(q_ref, k_ref, v_ref, o_ref, m_ref, l_ref, acc_ref, *, tq, tk, td, hp):
    qi = pl.program_id(2)
    m_ref[...] = jnp.full(m_ref.shape, NEG, F32)
    l_ref[...] = jnp.zeros(l_ref.shape, F32)
    acc_ref[...] = jnp.zeros(acc_ref.shape, F32)

    def tile(off, width, r0, diag):
        for h in range(hp):
            q = q_ref[0, r0:tq, h * QK_PAD:(h + 1) * QK_PAD]
            s = _dot_nt(q, k_ref[0, pl.ds(off, width), h * QK_PAD:(h + 1) * QK_PAD])
            if diag is not None:
                row = lax.broadcasted_iota(jnp.int32, s.shape, 0) + r0
                col = lax.broadcasted_iota(jnp.int32, s.shape, 1) + diag
                s = jnp.where(col <= row, s, NEG)
            _flash_update(s, v_ref[0, pl.ds(off, width), h * V_HEAD:(h + 1) * V_HEAD],
                          m_ref, l_ref, acc_ref, slice(h * tq + r0, (h + 1) * tq))

    def body(c, carry):
        tile(pl.multiple_of(c * tk, tk), tk, 0, None)
        return carry

    lax.fori_loop(0, qi * (tq // tk), body, 0)
    for d in range(tq // td):
        tile(pl.multiple_of(qi * tq + d * td, td), td, d * td, d * td)
    for h in range(hp):
        o_ref[0, :, h * V_HEAD:(h + 1) * V_HEAD] = _flash_finish(
            l_ref, acc_ref, slice(h * tq, (h + 1) * tq)).astype(o_ref.dtype)


def _monotone_key(x):
    bits = lax.bitcast_convert_type(x, jnp.int32)
    return bits ^ ((bits >> 31) & INT_MAX)


def _dsa_kernel(qa_ref, qi_ref, w_ref, kext_ref, v_ref, ke_ref, ko_ref, aug_ref, o_ref,
                key_ref, lhs_ref, qs_ref, wb_ref, m_ref, l_ref, acc_ref,
                *, q_blk, topk, chunk, group, ones, blind):
    if ones:
        l_ref = None
    qb = pl.program_id(1)
    n_chunks = (qb * q_blk + q_blk + chunk - 1) // chunk
    rep = chunk // LANES
    n_pairs = IDX_HEADS // 2

    for h in range(A_HEADS):
        lhs_ref[h * q_blk:(h + 1) * q_blk, 0:A_HEAD_DIM] = qa_ref[0, :, h * A_HEAD_DIM:(h + 1) * A_HEAD_DIM]
    lhs_ref[:, A_HEAD_DIM:2 * A_HEAD_DIM] = aug_ref[...]
    for j in range(n_pairs):
        qs_ref[j * q_blk:(j + 1) * q_blk, :] = qi_ref[0, :, j * LANES:(j + 1) * LANES]
    for h in range(IDX_HEADS):
        wb_ref[h] = jnp.broadcast_to(w_ref[0, :, h:h + 1], (q_blk, LANES))

    t_col = qb * q_blk + lax.broadcasted_iota(jnp.int32, (q_blk, 1), 0)

    def score_chunk(c, carry):
        off = pl.multiple_of(c * chunk, chunk)
        qs = qs_ref[...]
        se = _dot_nt(qs, ke_ref[0, pl.ds(off, chunk), :])
        so = _dot_nt(qs, ko_ref[0, pl.ds(off, chunk), :])
        acc = jnp.zeros((q_blk, chunk), F32)
        for j in range(n_pairs):
            we = jnp.concatenate([wb_ref[2 * j]] * rep, axis=1)
            wo = jnp.concatenate([wb_ref[2 * j + 1]] * rep, axis=1)
            acc = acc + we * jnp.maximum(se[j * q_blk:(j + 1) * q_blk], 0.0)
            acc = acc + wo * jnp.maximum(so[j * q_blk:(j + 1) * q_blk], 0.0)
        s_pos = off + lax.broadcasted_iota(jnp.int32, (q_blk, chunk), 1)
        key_ref[:, pl.ds(off, chunk)] = jnp.where(s_pos <= t_col, _monotone_key(acc), INT_MIN)
        return carry

    lax.fori_loop(0, n_chunks, score_chunk, 0)

    kf = float(topk)

    def count_ge(th):
        thb = jnp.broadcast_to(th, (q_blk, LANES))

        def body(c, cnt):
            off = pl.multiple_of(c * chunk, chunk)
            blk = key_ref[:, pl.ds(off, chunk)]
            for j in range(rep):
                cnt = cnt + jnp.where(blk[:, j * LANES:(j + 1) * LANES] >= thb, 1.0, 0.0)
            return cnt

        cnt = lax.fori_loop(0, n_chunks, body, jnp.zeros((q_blk, LANES), F32))
        return jnp.sum(cnt, axis=-1, keepdims=True)

    n_causal = (t_col + 1).astype(F32)
    few = n_causal <= kf
    lo0 = jnp.full((q_blk, 1), INT_MIN + 1, jnp.int32)
    hi0 = jnp.where(few, INT_MIN + 2, INT_MAX).astype(jnp.int32)
    cnt_lo0 = n_causal
    cnt_hi0 = jnp.zeros((q_blk, 1), F32)

    def active(lo, hi, cnt_lo):
        return jnp.logical_and(lo + 1 < hi, cnt_lo != kf)

    def bis_step(st):
        lo, hi, cnt_lo, cnt_hi = st
        act = active(lo, hi, cnt_lo)
        mid = (lo & hi) + ((lo ^ hi) >> 1)
        c = count_ge(mid)
        ge = jnp.logical_and(act, c >= kf)
        lt = jnp.logical_and(act, c < kf)
        return (jnp.where(ge, mid, lo), jnp.where(lt, mid, hi), jnp.where(ge, c, cnt_lo), jnp.where(lt, c, cnt_hi))

    def bis_cond(st):
        it, lo, hi, cnt_lo, cnt_hi = st
        n_act = jnp.max(jnp.where(active(lo, hi, cnt_lo), 1, 0))
        return jnp.logical_and(it < 32 - blind + 2, n_act > 0)

    st = lax.fori_loop(0, blind, lambda _, s_: bis_step(s_), (lo0, hi0, cnt_lo0, cnt_hi0))
    _, v_key, _, cnt_lo, cnt_hi = lax.while_loop(
        bis_cond, lambda s_: (s_[0] + 1,) + bis_step(s_[1:]), (jnp.int32(0),) + st)

    tied = jnp.logical_and(cnt_lo > kf, jnp.logical_not(few))
    need = kf - cnt_hi
    n_tied = jnp.max(jnp.where(tied, 1, 0))

    def count_tie_le(jj):
        vb = jnp.broadcast_to(v_key, (q_blk, LANES))
        jb = jnp.broadcast_to(jj, (q_blk, LANES))
        lane = lax.broadcasted_iota(jnp.int32, (q_blk, LANES), 1)

        def body(c, cnt):
            off = pl.multiple_of(c * chunk, chunk)
            blk = key_ref[:, pl.ds(off, chunk)]
            for j in range(rep):
                in_range = jnp.where(off + j * LANES + lane <= jb, 1.0, 0.0)
                cnt = cnt + jnp.where(blk[:, j * LANES:(j + 1) * LANES] == vb, in_range, 0.0)
            return cnt

        cnt = lax.fori_loop(0, n_chunks, body, jnp.zeros((q_blk, LANES), F32))
        return jnp.sum(cnt, axis=-1, keepdims=True)

    def tie_search(_):
        def body(_, st):
            lo, hi = st
            mid = (lo + hi) >> 1
            ok = count_tie_le(mid) >= need
            upd = lo + 1 < hi
            return (jnp.where(jnp.logical_and(upd, jnp.logical_not(ok)), mid, lo),
                    jnp.where(jnp.logical_and(upd, ok), mid, hi))

        lo = jnp.full((q_blk, 1), -1, jnp.int32)
        hi = jnp.broadcast_to(t_col, (q_blk, 1))
        _, hi = lax.fori_loop(0, 14, body, (lo, hi))
        return jnp.where(tied, hi, INT_MAX)

    j_max = lax.cond(n_tied > 0, tie_search, lambda _: jnp.full((q_blk, 1), INT_MAX, jnp.int32), 0)

    m_ref[...] = jnp.full(m_ref.shape, NEG, F32)
    acc_ref[...] = jnp.zeros(acc_ref.shape, F32)
    if l_ref is not None:
        l_ref[...] = jnp.zeros(l_ref.shape, F32)
    rows = group * q_blk
    n_groups = A_HEADS // group

    def attn_chunk(c, carry):
        off = pl.multiple_of(c * chunk, chunk)
        keys = key_ref[:, pl.ds(off, chunk)]
        s_pos = off + lax.broadcasted_iota(jnp.int32, (q_blk, chunk), 1)
        sel = jnp.where(keys > v_key, 1, jnp.where(keys == v_key, jnp.where(s_pos <= j_max, 1, 0), 0))
        bias = jnp.where(sel > 0, 0.0, NEG)
        kc = kext_ref[0, pl.ds(off, chunk), :]
        vc = v_ref[0, pl.ds(off, chunk), :]
        for g in range(n_groups):
            r0 = g * rows
            s = _dot_nt(lhs_ref[r0:r0 + rows, :], kc)
            s = (s.reshape(group, q_blk, chunk) + bias[None]).reshape(rows, chunk)
            _flash_update(s, vc, m_ref, l_ref, acc_ref, slice(r0, r0 + rows))
        return carry

    lax.fori_loop(0, n_chunks, attn_chunk, 0)
    for h in range(A_HEADS):
        r0 = h * q_blk
        o_ref[0, :, h * A_HEAD_DIM:(h + 1) * A_HEAD_DIM] = _flash_finish(
            l_ref, acc_ref, slice(r0, r0 + q_blk)).astype(o_ref.dtype)


def _merge_kernel(a_ref, b_ref, wa_ref, wb_ref, g0_ref, g1_ref, o_ref):
    ya = _dot(a_ref[...], wa_ref[...])
    yb = _dot(b_ref[...], wb_ref[...])
    o_ref[...] = (g0_ref[...] * ya + g1_ref[...] * yb).astype(o_ref.dtype)


def _proj_ln_kernel(x_ref, a_ref, w_ref, g_ref, b_ref, o_ref, obf_ref):
    y = DEEPNORM_ALPHA * x_ref[...] + _dot(a_ref[...], w_ref[...])
    out = _layer_norm(y, g_ref[...], b_ref[...])
    o_ref[...] = out
    obf_ref[...] = out.astype(BF16)


def _ffn_up_kernel(x_ref, xp_ref, wg_ref, wv_ref, cwg_ref, cwv_ref, cbg_ref, cbv_ref, o_ref, *, tm, seq):
    i = pl.program_id(1)
    first = (i * tm) % seq == 0
    x = x_ref[...]
    xp = xp_ref[...]
    row = lax.broadcasted_iota(jnp.int32, (tm, 1), 0)

    def conv(w_ref, cw_ref, cb_ref):
        h = _dot(x, w_ref[...])
        hp = jnp.where(first, 0.0, _dot(xp, w_ref[...]))
        hm1 = jnp.where(row == 0, hp[7:8, :], pltpu.roll(h, 1, 0))
        hm2 = jnp.where(row == 0, hp[6:7, :], jnp.where(row == 1, hp[7:8, :], pltpu.roll(h, 2, 0)))
        return cw_ref[0:1, :] * hm2 + cw_ref[1:2, :] * hm1 + cw_ref[2:3, :] * h + cb_ref[...]

    hg = conv(wg_ref, cwg_ref, cbg_ref)
    hv = conv(wv_ref, cwv_ref, cbv_ref)
    o_ref[...] = (hg * jax.nn.sigmoid(hg) * hv).astype(o_ref.dtype)


def _ple_kernel(x_ref, p_ref, wpg_ref, bpg_ref, wpe_ref, o_ref):
    gate = jax.nn.sigmoid(_dot(x_ref[...], wpg_ref[...]) + bpg_ref[...])
    o_ref[...] = gate * _dot(p_ref[...], wpe_ref[...])


def _down_ln_kernel(x_ref, a_ref, w_ref, ple_ref, g_ref, b_ref, o_ref, acc_ref):
    k = pl.program_id(1)

    @pl.when(k == 0)
    def _():
        acc_ref[...] = DEEPNORM_ALPHA * x_ref[...] + ple_ref[...]

    acc_ref[...] += _dot(a_ref[...], w_ref[...])

    @pl.when(k == pl.num_programs(1) - 1)
    def _():
        o_ref[...] = _layer_norm(acc_ref[...], g_ref[...], b_ref[...])


def _mla_attention(q, k, v, *, tq=1024, tk=1024, td=256, hp=2, flags=None):
    bsz, seq, _ = q.shape
    assert tq % tk == 0 and tq % td == 0
    return pl.pallas_call(
        functools.partial(_mla_attn_kernel, tq=tq, tk=tk, td=td, hp=hp),
        grid=(bsz, B_HEADS // hp, seq // tq),
        in_specs=[pl.BlockSpec((1, tq, hp * QK_PAD), lambda b, h, i: (b, i, h)),
                  pl.BlockSpec((1, seq, hp * QK_PAD), lambda b, h, i: (b, 0, h)),
                  pl.BlockSpec((1, seq, hp * V_HEAD), lambda b, h, i: (b, 0, h))],
        out_specs=pl.BlockSpec((1, tq, hp * V_HEAD), lambda b, h, i: (b, i, h)),
        out_shape=jax.ShapeDtypeStruct((bsz, seq, B_HEADS * V_HEAD), BF16),
        scratch_shapes=[pltpu.VMEM((hp * tq, LANES), F32), pltpu.VMEM((hp * tq, LANES), F32),
                        pltpu.VMEM((hp * tq, V_HEAD), F32)],
        compiler_params=_cparams(("parallel", "parallel", "arbitrary"), flags),
        name="mla_attn",
    )(q, k, v)


def _dsa_attention(qa, qidx, w_idx, kext, v_a, k_idx, aug, *, topk, q_blk=128, chunk=1024, group=8, ones=False,
                   blind=20, flags=None):
    bsz, seq, _ = qa.shape
    zk = jnp.zeros_like(k_idx)
    k_even = jnp.concatenate([k_idx, zk], axis=-1)
    k_odd = jnp.concatenate([zk, k_idx], axis=-1)
    aug_rows = jnp.repeat(aug, q_blk, axis=0)
    v_in = jnp.concatenate([v_a, jnp.ones_like(v_a)], axis=-1) if ones else v_a
    vw = v_in.shape[-1]
    return pl.pallas_call(
        functools.partial(_dsa_kernel, q_blk=q_blk, topk=topk, chunk=chunk, group=group, ones=ones, blind=blind),
        grid=(bsz, seq // q_blk),
        in_specs=[pl.BlockSpec((1, q_blk, A_HEADS * A_HEAD_DIM), lambda b, i: (b, i, 0)),
                  pl.BlockSpec((1, q_blk, IDX_HEADS * IDX_DIM), lambda b, i: (b, i, 0)),
                  pl.BlockSpec((1, q_blk, IDX_HEADS), lambda b, i: (b, i, 0)),
                  pl.BlockSpec((1, seq, 2 * A_HEAD_DIM), lambda b, i: (b, 0, 0)),
                  pl.BlockSpec((1, seq, vw), lambda b, i: (b, 0, 0)),
                  pl.BlockSpec((1, seq, LANES), lambda b, i: (b, 0, 0)),
                  pl.BlockSpec((1, seq, LANES), lambda b, i: (b, 0, 0)),
                  pl.BlockSpec((A_HEADS * q_blk, LANES), lambda b, i: (0, 0))],
        out_specs=pl.BlockSpec((1, q_blk, A_HEADS * A_HEAD_DIM), lambda b, i: (b, i, 0)),
        out_shape=jax.ShapeDtypeStruct((bsz, seq, A_HEADS * A_HEAD_DIM), BF16),
        scratch_shapes=[pltpu.VMEM((q_blk, seq), jnp.int32),
                        pltpu.VMEM((A_HEADS * q_blk, 2 * A_HEAD_DIM), BF16),
                        pltpu.VMEM((IDX_HEADS // 2 * q_blk, LANES), BF16),
                        pltpu.VMEM((IDX_HEADS, q_blk, LANES), F32),
                        pltpu.VMEM((A_HEADS * q_blk, LANES), F32),
                        pltpu.VMEM((8, LANES) if ones else (A_HEADS * q_blk, LANES), F32),
                        pltpu.VMEM((A_HEADS * q_blk, vw), F32)],
        compiler_params=_cparams(("parallel", "arbitrary"), flags),
        name="dsa_attn",
    )(qa, qidx, w_idx, kext, v_in, k_even, k_odd, aug_rows)


def _ffn_up(x1b, w_up_b, cw, cb, seq, *, tm=1024, tn=512):
    m, d = x1b.shape
    nft = D_FF // tn
    return pl.pallas_call(
        functools.partial(_ffn_up_kernel, tm=tm, seq=seq),
        grid=(nft, m // tm),
        in_specs=[pl.BlockSpec((tm, d), lambda j, i: (i, 0)),
                  pl.BlockSpec((8, d), lambda j, i: (jnp.maximum(i * (tm // 8) - 1, 0), 0)),
                  pl.BlockSpec((d, tn), lambda j, i: (0, j)),
                  pl.BlockSpec((d, tn), lambda j, i: (0, j + nft)),
                  pl.BlockSpec((CONV_W, tn), lambda j, i: (0, j)),
                  pl.BlockSpec((CONV_W, tn), lambda j, i: (0, j + nft)),
                  pl.BlockSpec((1, tn), lambda j, i: (0, j)),
                  pl.BlockSpec((1, tn), lambda j, i: (0, j + nft))],
        out_specs=pl.BlockSpec((tm, tn), lambda j, i: (i, j)),
        out_shape=jax.ShapeDtypeStruct((m, D_FF), BF16),
        compiler_params=_cparams(("parallel", "parallel")),
        name="ffn_up_conv",
    )(x1b, x1b, w_up_b, w_up_b, cw, cw, cb, cb)


def _ffn_down(x1, act, w_down_b, ple, g, b, *, tm=256, tk=D_FF):
    m, d = x1.shape
    return pl.pallas_call(
        _down_ln_kernel,
        grid=(m // tm, D_FF // tk),
        in_specs=[pl.BlockSpec((tm, d), lambda i, k: (i, 0)),
                  pl.BlockSpec((tm, tk), lambda i, k: (i, k)),
                  pl.BlockSpec((tk, d), lambda i, k: (k, 0)),
                  pl.BlockSpec((tm, d), lambda i, k: (i, 0)),
                  pl.BlockSpec((1, d), lambda i, k: (0, 0)),
                  pl.BlockSpec((1, d), lambda i, k: (0, 0))],
        out_specs=pl.BlockSpec((tm, d), lambda i, k: (i, 0)),
        out_shape=jax.ShapeDtypeStruct((m, d), F32),
        scratch_shapes=[pltpu.VMEM((tm, d), F32)],
        compiler_params=_cparams(("parallel", "arbitrary")),
        name="ffn_down_ln2",
    )(x1, act, w_down_b, ple, g, b)


def _split_bf16(x, parts):
    out = []
    rem = x
    for _ in range(parts):
        piece = rem.astype(BF16)
        out.append(piece)
        rem = rem - piece.astype(F32)
    return out


def kernel(x, p, w_in, q_a_norm, kv_a_norm, w_uq, w_uk, w_uv, w_branch_a, w_branch_b, w_o,
           ln1_g, ln1_b, w_up, conv_w, conv_b, w_down, w_pg, b_pg, w_pe, ln2_g, ln2_b):
    bsz, seq, d = x.shape
    m = bsz * seq
    topk = min(INDEX_TOPK_MAX, seq // 4)
    row = lambda a: a.reshape(1, -1)

    wi = w_in[0]
    o_qa, o_ka, o_va = 0, 2048, 2176
    o_qi, o_ki, o_wi, o_cq, o_ckv, o_kpe, o_gate = 2304, 3328, 3392, 3408, 3920, 4432, 4496
    w_qa = wi[:, o_qa:o_ka].astype(BF16)
    w_kv = wi[:, o_ka:o_qi].astype(BF16)
    w_qi = wi[:, o_qi:o_ki].astype(BF16)
    w_misc = jnp.pad(wi[:, o_ki:o_cq], ((0, 0), (0, LANES - (o_cq - o_ki)))).astype(BF16)
    w_lat = jnp.pad(wi[:, o_cq:o_gate], ((0, 0), (0, LANES - QK_ROPE))).astype(BF16)
    w_gate = wi[:, o_gate:].astype(BF16)

    wq = jnp.pad(w_uq[0], ((0, 0), (0, 0), (0, QK_PAD - QK_NOPE - QK_ROPE)))
    wq = wq.reshape(Q_LORA, B_HEADS * QK_PAD).astype(BF16)
    wk = w_uk[0].reshape(KV_LORA, B_HEADS * QK_NOPE).astype(BF16)
    wv = w_uv[0].reshape(KV_LORA, B_HEADS * V_HEAD).astype(BF16)

    pos = jnp.arange(seq, dtype=F32)
    inv_freq = jnp.power(ROPE_THETA, -jnp.arange(0, QK_ROPE, 2, dtype=F32) / QK_ROPE)
    ang = pos[:, None] * inv_freq[None, :]
    cos, sin = jnp.cos(ang), jnp.sin(ang)
    zpad = jnp.zeros((seq, LANES - QK_ROPE), F32)
    cos2 = jnp.concatenate([cos, cos, zpad], axis=1)
    sin2 = jnp.concatenate([-sin, sin, zpad], axis=1)

    slopes = LOG2E * jnp.exp2(-8.0 * jnp.arange(1, A_HEADS + 1, dtype=F32) / A_HEADS)
    sl = _split_bf16(slopes, 3)
    aug = jnp.stack([s_ * 64.0 for s_ in sl] + sl, axis=1).astype(BF16)
    aug = jnp.pad(aug, ((0, 0), (0, LANES - 6)))
    s_int = jnp.arange(seq, dtype=jnp.int32)
    s_hi = (s_int // 64).astype(BF16)
    s_lo = (s_int % 64).astype(BF16)
    kaug = jnp.stack([s_hi, s_hi, s_hi, s_lo, s_lo, s_lo], axis=1)
    kaug = jnp.pad(kaug, ((0, 0), (0, LANES - 6)))

    x2 = x.reshape(m, d)
    xb = x2.astype(BF16)

    qa = _matmul(xb, w_qa, BF16, scale=LOG2E * A_HEAD_DIM ** -0.5)
    kv = _matmul(xb, w_kv, BF16)
    qidx = _matmul(xb, w_qi, BF16)
    misc = _matmul(xb, w_misc, F32)
    lat = _matmul(xb, w_lat, F32, tn=1152)
    gates = _matmul(xb, w_gate, F32, act="sigmoid")

    tm = 512
    nst = seq // tm
    q_mla = pl.pallas_call(
        functools.partial(_mla_q_kernel, scale=LOG2E * (QK_NOPE + QK_ROPE) ** -0.5),
        grid=(m // tm,),
        in_specs=[pl.BlockSpec((tm, Q_LORA), lambda i: (i, 0)),
                  pl.BlockSpec((1, Q_LORA), lambda i: (0, 0)),
                  pl.BlockSpec((Q_LORA, B_HEADS * QK_PAD), lambda i: (0, 0)),
                  pl.BlockSpec((tm, LANES), lambda i: (i % nst, 0)),
                  pl.BlockSpec((tm, LANES), lambda i: (i % nst, 0))],
        out_specs=pl.BlockSpec((tm, B_HEADS * QK_PAD), lambda i: (i, 0)),
        out_shape=jax.ShapeDtypeStruct((m, B_HEADS * QK_PAD), BF16),
        compiler_params=_cparams(("parallel",)),
        name="mla_q_proj",
    )(lat, row(q_a_norm[0]), wq, cos2, sin2)

    k_mla, v_mla = pl.pallas_call(
        _mla_kv_kernel,
        grid=(m // tm,),
        in_specs=[pl.BlockSpec((tm, KV_LORA), lambda i: (i, 1)),
                  pl.BlockSpec((tm, LANES), lambda i: (i, (Q_LORA + KV_LORA) // LANES)),
                  pl.BlockSpec((1, KV_LORA), lambda i: (0, 0)),
                  pl.BlockSpec((KV_LORA, B_HEADS * QK_NOPE), lambda i: (0, 0)),
                  pl.BlockSpec((KV_LORA, B_HEADS * V_HEAD), lambda i: (0, 0)),
                  pl.BlockSpec((tm, LANES), lambda i: (i % nst, 0)),
                  pl.BlockSpec((tm, LANES), lambda i: (i % nst, 0))],
        out_specs=[pl.BlockSpec((tm, B_HEADS * QK_PAD), lambda i: (i, 0)),
                   pl.BlockSpec((tm, B_HEADS * V_HEAD), lambda i: (i, 0))],
        out_shape=[jax.ShapeDtypeStruct((m, B_HEADS * QK_PAD), BF16),
                   jax.ShapeDtypeStruct((m, B_HEADS * V_HEAD), BF16)],
        compiler_params=_cparams(("parallel",)),
        name="mla_kv_proj",
    )(lat, lat, row(kv_a_norm[0]), wk, wv, cos2, sin2)

    attn_b = _mla_attention(q_mla.reshape(bsz, seq, -1), k_mla.reshape(bsz, seq, -1), v_mla.reshape(bsz, seq, -1))

    k_a = kv[:, :A_HEAD_DIM].reshape(bsz, seq, A_HEAD_DIM)
    v_a = kv[:, A_HEAD_DIM:].reshape(bsz, seq, A_HEAD_DIM)
    kext = jnp.concatenate([k_a, jnp.broadcast_to(kaug[None], (bsz, seq, LANES))], axis=-1)
    k_idx = misc[:, :IDX_DIM].astype(BF16).reshape(bsz, seq, IDX_DIM)
    w_idx = misc[:, IDX_DIM:IDX_DIM + IDX_HEADS].reshape(bsz, seq, IDX_HEADS)
    attn_a = _dsa_attention(qa.reshape(bsz, seq, -1), qidx.reshape(bsz, seq, -1), w_idx, kext, v_a, k_idx, aug,
                            topk=topk)

    tmm, tnm = 1024, 512
    merged = pl.pallas_call(
        _merge_kernel,
        grid=(d // tnm, m // tmm),
        in_specs=[pl.BlockSpec((tmm, d), lambda j, i: (i, 0)),
                  pl.BlockSpec((tmm, d), lambda j, i: (i, 0)),
                  pl.BlockSpec((d, tnm), lambda j, i: (0, j)),
                  pl.BlockSpec((d, tnm), lambda j, i: (0, j)),
                  pl.BlockSpec((tmm, tnm), lambda j, i: (i, j)),
                  pl.BlockSpec((tmm, tnm), lambda j, i: (i, j + d // tnm))],
        out_specs=pl.BlockSpec((tmm, tnm), lambda j, i: (i, j)),
        out_shape=jax.ShapeDtypeStruct((m, d), BF16),
        compiler_params=_cparams(("parallel", "parallel")),
        name="branch_merge",
    )(attn_a.reshape(m, d), attn_b.reshape(m, d), w_branch_a[0].astype(BF16), w_branch_b[0].astype(BF16),
      gates, gates)

    tml = 512
    x1, x1b = pl.pallas_call(
        _proj_ln_kernel,
        grid=(m // tml,),
        in_specs=[pl.BlockSpec((tml, d), lambda i: (i, 0)),
                  pl.BlockSpec((tml, d), lambda i: (i, 0)),
                  pl.BlockSpec((d, d), lambda i: (0, 0)),
                  pl.BlockSpec((1, d), lambda i: (0, 0)),
                  pl.BlockSpec((1, d), lambda i: (0, 0))],
        out_specs=[pl.BlockSpec((tml, d), lambda i: (i, 0)), pl.BlockSpec((tml, d), lambda i: (i, 0))],
        out_shape=[jax.ShapeDtypeStruct((m, d), F32), jax.ShapeDtypeStruct((m, d), BF16)],
        compiler_params=_cparams(("parallel",)),
        name="oproj_ln1",
    )(x2, merged, w_o[0].astype(BF16), row(ln1_g[0]), row(ln1_b[0]))

    act = _ffn_up(x1b, w_up[0].astype(BF16), conv_w[0], row(conv_b[0]), seq)

    tnp = 1024
    ple = pl.pallas_call(
        _ple_kernel,
        grid=(d // tnp, m // tmm),
        in_specs=[pl.BlockSpec((tmm, d), lambda j, i: (i, 0)),
                  pl.BlockSpec((tmm, PLE_DIM), lambda j, i: (i, 0)),
                  pl.BlockSpec((d, tnp), lambda j, i: (0, j)),
                  pl.BlockSpec((1, tnp), lambda j, i: (0, j)),
                  pl.BlockSpec((PLE_DIM, tnp), lambda j, i: (0, j))],
        out_specs=pl.BlockSpec((tmm, tnp), lambda j, i: (i, j)),
        out_shape=jax.ShapeDtypeStruct((m, d), F32),
        compiler_params=_cparams(("parallel", "parallel")),
        name="ple_gate",
    )(x1b, p[0].reshape(m, PLE_DIM).astype(BF16), w_pg[0].astype(BF16), row(b_pg[0]), w_pe[0].astype(BF16))

    out = _ffn_down(x1, act, w_down[0].astype(BF16), ple, row(ln2_g[0]), row(ln2_b[0]))
    return out.reshape(bsz, seq, d)
```

```python
import functools

import jax
import jax.numpy as jnp
import numpy as np
from jax import lax
from jax.experimental import pallas as pl
from jax.experimental.pallas import tpu as pltpu

F32 = jnp.float32
BF16 = jnp.bfloat16

D_MODEL = 2048
A_HEADS = 16
A_HEAD_DIM = 128
IDX_HEADS = 16
IDX_DIM = 64
INDEX_TOPK_MAX = 256
B_HEADS = 16
Q_LORA = 512
KV_LORA = 512
QK_NOPE = 128
QK_ROPE = 64
V_HEAD = 128
ROPE_THETA = 10000.0
D_FF = 5632
CONV_W = 3
PLE_DIM = 256
LN_EPS = 1e-5
RMS_EPS = 1e-6
NEG = -1e30
DEEPNORM_ALPHA = 2.0 ** 0.25
LOG2E = 1.4426950408889634

LANES = 128
QK_PAD = 256
VE = 2 * LANES
VMEM_LIMIT = 56 * 1024 * 1024
INT_MIN = -2 ** 31
INT_MAX = 2 ** 31 - 1


def _cparams(sem, flags=None):
    return pltpu.CompilerParams(dimension_semantics=sem, vmem_limit_bytes=VMEM_LIMIT, flags=flags)


def _dot(a, b):
    return jnp.dot(a, b, preferred_element_type=F32)


def _dot_nt(a, b):
    return lax.dot_general(a, b, (((1,), (1,)), ((), ())), preferred_element_type=F32)


def _layer_norm(y, g, b):
    mu = jnp.mean(y, axis=-1, keepdims=True)
    d = y - mu
    var = jnp.mean(d * d, axis=-1, keepdims=True)
    return d * lax.rsqrt(var + LN_EPS) * g + b


def _rms_norm(c, g):
    return c * lax.rsqrt(jnp.mean(c * c, axis=-1, keepdims=True) + RMS_EPS) * g


def _rope_lanes(pe, cos2, sin2):
    lane = lax.broadcasted_iota(jnp.int32, pe.shape, 1)
    swapped = jnp.where(lane < QK_ROPE // 2, pltpu.roll(pe, LANES - QK_ROPE // 2, 1),
                        pltpu.roll(pe, QK_ROPE // 2, 1))
    return pe * cos2 + swapped * sin2


def _mm_kernel(x_ref, w_ref, o_ref, *, scale, act):
    y = _dot(x_ref[...], w_ref[...])
    if scale != 1.0:
        y = y * scale
    if act == "sigmoid":
        y = jax.nn.sigmoid(y)
    o_ref[...] = y.astype(o_ref.dtype)


def _matmul(x, w, out_dtype, *, scale=1.0, act=None, tm=1024, tn=1024):
    m, k = x.shape
    n = w.shape[1]
    tn = min(tn, n)
    assert m % tm == 0 and n % tn == 0
    return pl.pallas_call(
        functools.partial(_mm_kernel, scale=scale, act=act),
        grid=(n // tn, m // tm),
        in_specs=[pl.BlockSpec((tm, k), lambda j, i: (i, 0)),
                  pl.BlockSpec((k, tn), lambda j, i: (0, j))],
        out_specs=pl.BlockSpec((tm, tn), lambda j, i: (i, j)),
        out_shape=jax.ShapeDtypeStruct((m, n), out_dtype),
        compiler_params=_cparams(("parallel", "parallel")),
        name="proj_mm",
    )(x, w)


def _mla_q_kernel(c_ref, g_ref, w_ref, cos_ref, sin_ref, o_ref, *, scale):
    y = _rms_norm(c_ref[...], g_ref[...]).astype(BF16)
    q = _dot(y, w_ref[...])
    cos2 = cos_ref[...]
    sin2 = sin_ref[...]
    for h in range(B_HEADS):
        base = h * QK_PAD
        o_ref[:, base:base + QK_NOPE] = (q[:, base:base + QK_NOPE] * scale).astype(BF16)
        pe = q[:, base + QK_NOPE:base + QK_PAD]
        o_ref[:, base + QK_NOPE:base + QK_PAD] = (_rope_lanes(pe, cos2, sin2) * scale).astype(BF16)


def _mla_kv_kernel(c_ref, kpe_ref, g_ref, wk_ref, wv_ref, cos_ref, sin_ref, k_ref, v_ref):
    y = _rms_norm(c_ref[...], g_ref[...]).astype(BF16)
    kn = _dot(y, wk_ref[...])
    v_ref[...] = _dot(y, wv_ref[...]).astype(BF16)
    kpe = _rope_lanes(kpe_ref[...], cos_ref[...], sin_ref[...]).astype(BF16)
    for h in range(B_HEADS):
        base = h * QK_PAD
        k_ref[:, base:base + QK_NOPE] = kn[:, h * QK_NOPE:(h + 1) * QK_NOPE].astype(BF16)
        k_ref[:, base + QK_NOPE:base + QK_PAD] = kpe


def _flash_update(s, v_chunk, m_ref, l_ref, acc_ref, rows):
    nblk = s.shape[1] // LANES
    m_prev = m_ref[rows, :]
    m_new = jnp.maximum(m_prev, jnp.max(s, axis=-1, keepdims=True))
    alpha = jnp.exp2(m_prev - m_new)
    ps = [jnp.exp2(s[:, j * LANES:(j + 1) * LANES] - m_new) for j in range(nblk)]
    p = jnp.concatenate([x.astype(BF16) for x in ps], axis=1)
    if l_ref is None:
        acc_ref[rows, :] = jnp.concatenate([alpha, alpha], axis=1) * acc_ref[rows, :] + _dot(p, v_chunk)
    else:
        l_ref[rows, :] = alpha * l_ref[rows, :] + functools.reduce(lambda a, b: a + b, ps)
        acc_ref[rows, :] = alpha * acc_ref[rows, :] + _dot(p, v_chunk)
    m_ref[rows, :] = m_new


def _flash_finish(l_ref, acc_ref, rows):
    if l_ref is None:
        return acc_ref[rows, 0:LANES] / acc_ref[rows, LANES:VE]
    return acc_ref[rows, :] / jnp.sum(l_ref[rows, :], axis=-1, keepdims=True)


def _mla_attn_kernel(q_ref, k_ref, v_ref, o_ref, m_ref, l_ref, acc_ref, *, tq, tk, td, hp):
    qi = pl.program_id(2)
    m_ref[...] = jnp.full(m_ref.shape, NEG, F32)
    l_ref[...] = jnp.zeros(l_ref.shape, F32)
    acc_ref[...] = jnp.zeros(acc_ref.shape, F32)

    def tile(off, width, r0, diag):
        for h in range(hp):
            q = q_ref[0, r0:tq, h * QK_PAD:(h + 1) * QK_PAD]
            s = _dot_nt(q, k_ref[0, pl.ds(off, width), h * QK_PAD:(h + 1) * QK_PAD])
            if diag is not None:
                row = lax.broadcasted_iota(jnp.int32, s.shape, 0) + r0
                col = lax.broadcasted_iota(jnp.int32, s.shape, 1) + diag
                s = jnp.where(col <= row, s, NEG)
            _flash_update(s, v_ref[0, pl.ds(off, width), h * V_HEAD:(h + 1) * V_HEAD],
                          m_ref, l_ref, acc_ref, slice(h * tq + r0, (h + 1) * tq))

    def body(c, carry):
        tile(pl.multiple_of(c * tk, tk), tk, 0, None)
        return carry

    lax.fori_loop(0, qi * (tq // tk), body, 0)
    for d in range(tq // td):
        tile(pl.multiple_of(qi * tq + d * td, td), td, d * td, d * td)
    for h in range(hp):
        o_ref[0, :, h * V_HEAD:(h + 1) * V_HEAD] = _flash_finish(
            l_ref, acc_ref, slice(h * tq, (h + 1) * tq)).astype(o_ref.dtype)


def _monotone_key(x):
    bits = lax.bitcast_convert_type(x, jnp.int32)
    return bits ^ ((bits >> 31) & INT_MAX)


def _dsa_kernel(qa_ref, qi_ref, w_ref, kext_ref, v_ref, ke_ref, ko_ref, aug_ref, o_ref,
                key_ref, lhs_ref, qs_ref, wb_ref, m_ref, l_ref, acc_ref,
                *, q_blk, topk, chunk, group, ones, blind):
    if ones:
        l_ref = None
    qb = pl.program_id(1)
    n_chunks = (qb * q_blk + q_blk + chunk - 1) // chunk
    rep = chunk // LANES
    n_pairs = IDX_HEADS // 2

    for h in range(A_HEADS):
        lhs_ref[h * q_blk:(h + 1) * q_blk, 0:A_HEAD_DIM] = qa_ref[0, :, h * A_HEAD_DIM:(h + 1) * A_HEAD_DIM]
    lhs_ref[:, A_HEAD_DIM:2 * A_HEAD_DIM] = aug_ref[...]
    for j in range(n_pairs):
        qs_ref[j * q_blk:(j + 1) * q_blk, :] = qi_ref[0, :, j * LANES:(j + 1) * LANES]
    for h in range(IDX_HEADS):
        wb_ref[h] = jnp.broadcast_to(w_ref[0, :, h:h + 1], (q_blk, LANES))

    t_col = qb * q_blk + lax.broadcasted_iota(jnp.int32, (q_blk, 1), 0)

    def score_chunk(c, carry):
        off = pl.multiple_of(c * chunk, chunk)
        qs = qs_ref[...]
        se = _dot_nt(qs, ke_ref[0, pl.ds(off, chunk), :])
        so = _dot_nt(qs, ko_ref[0, pl.ds(off, chunk), :])
        acc = jnp.zeros((q_blk, chunk), F32)
        for j in range(n_pairs):
            we = jnp.concatenate([wb_ref[2 * j]] * rep, axis=1)
            wo = jnp.concatenate([wb_ref[2 * j + 1]] * rep, axis=1)
            acc = acc + we * jnp.maximum(se[j * q_blk:(j + 1) * q_blk], 0.0)
            acc = acc + wo * jnp.maximum(so[j * q_blk:(j + 1) * q_blk], 0.0)
        s_pos = off + lax.broadcasted_iota(jnp.int32, (q_blk, chunk), 1)
        key_ref[:, pl.ds(off, chunk)] = jnp.where(s_pos <= t_col, _monotone_key(acc), INT_MIN)
        return carry

    lax.fori_loop(0, n_chunks, score_chunk, 0)

    kf = float(topk)

    def count_ge(th):
        thb = jnp.broadcast_to(th, (q_blk, LANES))

        def body(c, cnt):
            off = pl.multiple_of(c * chunk, chunk)
            blk = key_ref[:, pl.ds(off, chunk)]
            for j in range(rep):
                cnt = cnt + jnp.where(blk[:, j * LANES:(j + 1) * LANES] >= thb, 1.0, 0.0)
            return cnt

        cnt = lax.fori_loop(0, n_chunks, body, jnp.zeros((q_blk, LANES), F32))
        return jnp.sum(cnt, axis=-1, keepdims=True)

    n_causal = (t_col + 1).astype(F32)
    few = n_causal <= kf
    lo0 = jnp.full((q_blk, 1), INT_MIN + 1, jnp.int32)
    hi0 = jnp.where(few, INT_MIN + 2, INT_MAX).astype(jnp.int32)
    cnt_lo0 = n_causal
    cnt_hi0 = jnp.zeros((q_blk, 1), F32)

    def active(lo, hi, cnt_lo):
        return jnp.logical_and(lo + 1 < hi, cnt_lo != kf)

    def bis_step(st):
        lo, hi, cnt_lo, cnt_hi = st
        act = active(lo, hi, cnt_lo)
        mid = (lo & hi) + ((lo ^ hi) >> 1)
        c = count_ge(mid)
        ge = jnp.logical_and(act, c >= kf)
        lt = jnp.logical_and(act, c < kf)
        return (jnp.where(ge, mid, lo), jnp.where(lt, mid, hi), jnp.where(ge, c, cnt_lo), jnp.where(lt, c, cnt_hi))

    def bis_cond(st):
        it, lo, hi, cnt_lo, cnt_hi = st
        n_act = jnp.max(jnp.where(active(lo, hi, cnt_lo), 1, 0))
        return jnp.logical_and(it < 32 - blind + 2, n_act > 0)

    st = lax.fori_loop(0, blind, lambda _, s_: bis_step(s_), (lo0, hi0, cnt_lo0, cnt_hi0))
    _, v_key, _, cnt_lo, cnt_hi = lax.while_loop(
        bis_cond, lambda s_: (s_[0] + 1,) + bis_step(s_[1:]), (jnp.int32(0),) + st)

    tied = jnp.logical_and(cnt_lo > kf, jnp.logical_not(few))
    need = kf - cnt_hi
    n_tied = jnp.max(jnp.where(tied, 1, 0))

    def count_tie_le(jj):
        vb = jnp.broadcast_to(v_key, (q_blk, LANES))
        jb = jnp.broadcast_to(jj, (q_blk, LANES))
        lane = lax.broadcasted_iota(jnp.int32, (q_blk, LANES), 1)

        def body(c, cnt):
            off = pl.multiple_of(c * chunk, chunk)
            blk = key_ref[:, pl.ds(off, chunk)]
            for j in range(rep):
                in_range = jnp.where(off + j * LANES + lane <= jb, 1.0, 0.0)
                cnt = cnt + jnp.where(blk[:, j * LANES:(j + 1) * LANES] == vb, in_range, 0.0)
            return cnt

        cnt = lax.fori_loop(0, n_chunks, body, jnp.zeros((q_blk, LANES), F32))
        return jnp.sum(cnt, axis=-1, keepdims=True)

    def tie_search(_):
        def body(_, st):
            lo, hi = st
            mid = (lo + hi) >> 1
            ok = count_tie_le(mid) >= need
            upd = lo + 1 < hi
            return (jnp.where(jnp.logical_and(upd, jnp.logical_not(ok)), mid, lo),
                    jnp.where(jnp.logical_and(upd, ok), mid, hi))

        lo = jnp.full((q_blk, 1), -1, jnp.int32)
        hi = jnp.broadcast_to(t_col, (q_blk, 1))
        _, hi = lax.fori_loop(0, 14, body, (lo, hi))
        return jnp.where(tied, hi, INT_MAX)

    j_max = lax.cond(n_tied > 0, tie_search, lambda _: jnp.full((q_blk, 1), INT_MAX, jnp.int32), 0)

    m_ref[...] = jnp.full(m_ref.shape, NEG, F32)
    acc_ref[...] = jnp.zeros(acc_ref.shape, F32)
    if l_ref is not None:
        l_ref[...] = jnp.zeros(l_ref.shape, F32)
    rows = group * q_blk
    n_groups = A_HEADS // group

    def attn_chunk(c, carry):
        off = pl.multiple_of(c * chunk, chunk)
        keys = key_ref[:, pl.ds(off, chunk)]
        s_pos = off + lax.broadcasted_iota(jnp.int32, (q_blk, chunk), 1)
        sel = jnp.where(keys > v_key, 1, jnp.where(keys == v_key, jnp.where(s_pos <= j_max, 1, 0), 0))
        bias = jnp.where(sel > 0, 0.0, NEG)
        kc = kext_ref[0, pl.ds(off, chunk), :]
        vc = v_ref[0, pl.ds(off, chunk), :]
        for g in range(n_groups):
            r0 = g * rows
            s = _dot_nt(lhs_ref[r0:r0 + rows, :], kc)
            s = (s.reshape(group, q_blk, chunk) + bias[None]).reshape(rows, chunk)
            _flash_update(s, vc, m_ref, l_ref, acc_ref, slice(r0, r0 + rows))
        return carry

    lax.fori_loop(0, n_chunks, attn_chunk, 0)
    for h in range(A_HEADS):
        r0 = h * q_blk
        o_ref[0, :, h * A_HEAD_DIM:(h + 1) * A_HEAD_DIM] = _flash_finish(
            l_ref, acc_ref, slice(r0, r0 + q_blk)).astype(o_ref.dtype)


def _merge_kernel(a_ref, b_ref, wa_ref, wb_ref, g0_ref, g1_ref, o_ref):
    ya = _dot(a_ref[...], wa_ref[...])
    yb = _dot(b_ref[...], wb_ref[...])
    o_ref[...] = (g0_ref[...] * ya + g1_ref[...] * yb).astype(o_ref.dtype)


def _proj_ln_kernel(x_ref, a_ref, w_ref, g_ref, b_ref, o_ref, obf_ref):
    y = DEEPNORM_ALPHA * x_ref[...] + _dot(a_ref[...], w_ref[...])
    out = _layer_norm(y, g_ref[...], b_ref[...])
    o_ref[...] = out
    obf_ref[...] = out.astype(BF16)


def _ffn_up_kernel(x_ref, wg_ref, wv_ref, cwg_ref, cwv_ref, cbg_ref, cbv_ref, o_ref, tail_g_ref, tail_v_ref,
                   *, tm, seq):
    i = pl.program_id(1)

    @pl.when((i * tm) % seq == 0)
    def _():
        tail_g_ref[...] = jnp.zeros(tail_g_ref.shape, F32)
        tail_v_ref[...] = jnp.zeros(tail_v_ref.shape, F32)

    x = x_ref[...]
    row = lax.broadcasted_iota(jnp.int32, (tm, 1), 0)

    def conv(w_ref, cw_ref, cb_ref, tail_ref):
        h = _dot(x, w_ref[...])
        hp = tail_ref[...]
        tail_ref[...] = h[tm - 8:tm, :]
        hm1 = jnp.where(row == 0, hp[7:8, :], pltpu.roll(h, 1, 0))
        hm2 = jnp.where(row == 0, hp[6:7, :], jnp.where(row == 1, hp[7:8, :], pltpu.roll(h, 2, 0)))
        return cw_ref[0:1, :] * hm2 + cw_ref[1:2, :] * hm1 + cw_ref[2:3, :] * h + cb_ref[...]

    hg = conv(wg_ref, cwg_ref, cbg_ref, tail_g_ref)
    hv = conv(wv_ref, cwv_ref, cbv_ref, tail_v_ref)
    o_ref[...] = (hg * jax.nn.sigmoid(hg) * hv).astype(o_ref.dtype)


def _ple_kernel(x_ref, p_ref, wpg_ref, bpg_ref, wpe_ref, o_ref):
    gate = jax.nn.sigmoid(_dot(x_ref[...], wpg_ref[...]) + bpg_ref[...])
    o_ref[...] = gate * _dot(p_ref[...], wpe_ref[...])


def _down_ln_kernel(x_ref, a_ref, w_ref, ple_ref, g_ref, b_ref, o_ref, acc_ref):
    k = pl.program_id(1)

    @pl.when(k == 0)
    def _():
        acc_ref[...] = DEEPNORM_ALPHA * x_ref[...] + ple_ref[...]

    acc_ref[...] += _dot(a_ref[...], w_ref[...])

    @pl.when(k == pl.num_programs(1) - 1)
    def _():
        o_ref[...] = _layer_norm(acc_ref[...], g_ref[...], b_ref[...])


def _mla_attention(q, k, v, *, tq=1024, tk=1024, td=256, hp=2, flags=None):
    bsz, seq, _ = q.shape
    assert tq % tk == 0 and tq % td == 0
    return pl.pallas_call(
        functools.partial(_mla_attn_kernel, tq=tq, tk=tk, td=td, hp=hp),
        grid=(bsz, B_HEADS // hp, seq // tq),
        in_specs=[pl.BlockSpec((1, tq, hp * QK_PAD), lambda b, h, i: (b, i, h)),
                  pl.BlockSpec((1, seq, hp * QK_PAD), lambda b, h, i: (b, 0, h)),
                  pl.BlockSpec((1, seq, hp * V_HEAD), lambda b, h, i: (b, 0, h))],
        out_specs=pl.BlockSpec((1, tq, hp * V_HEAD), lambda b, h, i: (b, i, h)),
        out_shape=jax.ShapeDtypeStruct((bsz, seq, B_HEADS * V_HEAD), BF16),
        scratch_shapes=[pltpu.VMEM((hp * tq, LANES), F32), pltpu.VMEM((hp * tq, LANES), F32),
                        pltpu.VMEM((hp * tq, V_HEAD), F32)],
        compiler_params=_cparams(("parallel", "parallel", "arbitrary"), flags),
        name="mla_attn",
    )(q, k, v)


def _dsa_attention(qa, qidx, w_idx, kext, v_a, k_idx, aug, *, topk, q_blk=128, chunk=1024, group=8, ones=False,
                   blind=20, flags=None):
    bsz, seq, _ = qa.shape
    zk = jnp.zeros_like(k_idx)
    k_even = jnp.concatenate([k_idx, zk], axis=-1)
    k_odd = jnp.concatenate([zk, k_idx], axis=-1)
    aug_rows = jnp.repeat(aug, q_blk, axis=0)
    v_in = jnp.concatenate([v_a, jnp.ones_like(v_a)], axis=-1) if ones else v_a
    vw = v_in.shape[-1]
    return pl.pallas_call(
        functools.partial(_dsa_kernel, q_blk=q_blk, topk=topk, chunk=chunk, group=group, ones=ones, blind=blind),
        grid=(bsz, seq // q_blk),
        in_specs=[pl.BlockSpec((1, q_blk, A_HEADS * A_HEAD_DIM), lambda b, i: (b, i, 0)),
                  pl.BlockSpec((1, q_blk, IDX_HEADS * IDX_DIM), lambda b, i: (b, i, 0)),
                  pl.BlockSpec((1, q_blk, IDX_HEADS), lambda b, i: (b, i, 0)),
                  pl.BlockSpec((1, seq, 2 * A_HEAD_DIM), lambda b, i: (b, 0, 0)),
                  pl.BlockSpec((1, seq, vw), lambda b, i: (b, 0, 0)),
                  pl.BlockSpec((1, seq, LANES), lambda b, i: (b, 0, 0)),
                  pl.BlockSpec((1, seq, LANES), lambda b, i: (b, 0, 0)),
                  pl.BlockSpec((A_HEADS * q_blk, LANES), lambda b, i: (0, 0))],
        out_specs=pl.BlockSpec((1, q_blk, A_HEADS * A_HEAD_DIM), lambda b, i: (b, i, 0)),
        out_shape=jax.ShapeDtypeStruct((bsz, seq, A_HEADS * A_HEAD_DIM), BF16),
        scratch_shapes=[pltpu.VMEM((q_blk, seq), jnp.int32),
                        pltpu.VMEM((A_HEADS * q_blk, 2 * A_HEAD_DIM), BF16),
                        pltpu.VMEM((IDX_HEADS // 2 * q_blk, LANES), BF16),
                        pltpu.VMEM((IDX_HEADS, q_blk, LANES), F32),
                        pltpu.VMEM((A_HEADS * q_blk, LANES), F32),
                        pltpu.VMEM((8, LANES) if ones else (A_HEADS * q_blk, LANES), F32),
                        pltpu.VMEM((A_HEADS * q_blk, vw), F32)],
        compiler_params=_cparams(("parallel", "arbitrary"), flags),
        name="dsa_attn",
    )(qa, qidx, w_idx, kext, v_in, k_even, k_odd, aug_rows)


def _ffn_up(x1b, w_up_b, cw, cb, seq, *, tm=1024, tn=512):
    m, d = x1b.shape
    nft = D_FF // tn
    return pl.pallas_call(
        functools.partial(_ffn_up_kernel, tm=tm, seq=seq),
        grid=(nft, m // tm),
        in_specs=[pl.BlockSpec((tm, d), lambda j, i: (i, 0)),
                  pl.BlockSpec((d, tn), lambda j, i: (0, j)),
                  pl.BlockSpec((d, tn), lambda j, i: (0, j + nft)),
                  pl.BlockSpec((CONV_W, tn), lambda j, i: (0, j)),
                  pl.BlockSpec((CONV_W, tn), lambda j, i: (0, j + nft)),
                  pl.BlockSpec((1, tn), lambda j, i: (0, j)),
                  pl.BlockSpec((1, tn), lambda j, i: (0, j + nft))],
        out_specs=pl.BlockSpec((tm, tn), lambda j, i: (i, j)),
        out_shape=jax.ShapeDtypeStruct((m, D_FF), BF16),
        scratch_shapes=[pltpu.VMEM((8, tn), F32), pltpu.VMEM((8, tn), F32)],
        compiler_params=_cparams(("parallel", "arbitrary")),
        name="ffn_up_conv",
    )(x1b, w_up_b, w_up_b, cw, cw, cb, cb)


def _ffn_down(x1, act, w_down_b, ple, g, b, *, tm=256, tk=D_FF):
    m, d = x1.shape
    return pl.pallas_call(
        _down_ln_kernel,
        grid=(m // tm, D_FF // tk),
        in_specs=[pl.BlockSpec((tm, d), lambda i, k: (i, 0)),
                  pl.BlockSpec((tm, tk), lambda i, k: (i, k)),
                  pl.BlockSpec((tk, d), lambda i, k: (k, 0)),
                  pl.BlockSpec((tm, d), lambda i, k: (i, 0)),
                  pl.BlockSpec((1, d), lambda i, k: (0, 0)),
                  pl.BlockSpec((1, d), lambda i, k: (0, 0))],
        out_specs=pl.BlockSpec((tm, d), lambda i, k: (i, 0)),
        out_shape=jax.ShapeDtypeStruct((m, d), F32),
        scratch_shapes=[pltpu.VMEM((tm, d), F32)],
        compiler_params=_cparams(("parallel", "arbitrary")),
        name="ffn_down_ln2",
    )(x1, act, w_down_b, ple, g, b)


def _split_bf16(x, parts):
    out = []
    rem = x
    for _ in range(parts):
        piece = rem.astype(BF16)
        out.append(piece)
        rem = rem - piece.astype(F32)
    return out


def kernel(x, p, w_in, q_a_norm, kv_a_norm, w_uq, w_uk, w_uv, w_branch_a, w_branch_b, w_o,
           ln1_g, ln1_b, w_up, conv_w, conv_b, w_down, w_pg, b_pg, w_pe, ln2_g, ln2_b):
    bsz, seq, d = x.shape
    m = bsz * seq
    topk = min(INDEX_TOPK_MAX, seq // 4)
    row = lambda a: a.reshape(1, -1)

    wi = w_in[0]
    o_qa, o_ka, o_va = 0, 2048, 2176
    o_qi, o_ki, o_wi, o_cq, o_ckv, o_kpe, o_gate = 2304, 3328, 3392, 3408, 3920, 4432, 4496
    w_qa = wi[:, o_qa:o_ka].astype(BF16)
    w_kv = wi[:, o_ka:o_qi].astype(BF16)
    w_qi = wi[:, o_qi:o_ki].astype(BF16)
    w_misc = jnp.pad(wi[:, o_ki:o_cq], ((0, 0), (0, LANES - (o_cq - o_ki)))).astype(BF16)
    w_lat = jnp.pad(wi[:, o_cq:o_gate], ((0, 0), (0, LANES - QK_ROPE))).astype(BF16)
    w_gate = wi[:, o_gate:].astype(BF16)

    wq = jnp.pad(w_uq[0], ((0, 0), (0, 0), (0, QK_PAD - QK_NOPE - QK_ROPE)))
    wq = wq.reshape(Q_LORA, B_HEADS * QK_PAD).astype(BF16)
    wk = w_uk[0].reshape(KV_LORA, B_HEADS * QK_NOPE).astype(BF16)
    wv = w_uv[0].reshape(KV_LORA, B_HEADS * V_HEAD).astype(BF16)

    pos = jnp.arange(seq, dtype=F32)
    inv_freq = jnp.power(ROPE_THETA, -jnp.arange(0, QK_ROPE, 2, dtype=F32) / QK_ROPE)
    ang = pos[:, None] * inv_freq[None, :]
    cos, sin = jnp.cos(ang), jnp.sin(ang)
    zpad = jnp.zeros((seq, LANES - QK_ROPE), F32)
    cos2 = jnp.concatenate([cos, cos, zpad], axis=1)
    sin2 = jnp.concatenate([-sin, sin, zpad], axis=1)

    slopes = LOG2E * jnp.exp2(-8.0 * jnp.arange(1, A_HEADS + 1, dtype=F32) / A_HEADS)
    sl = _split_bf16(slopes, 3)
    aug = jnp.stack([s_ * 64.0 for s_ in sl] + sl, axis=1).astype(BF16)
    aug = jnp.pad(aug, ((0, 0), (0, LANES - 6)))
    s_int = jnp.arange(seq, dtype=jnp.int32)
    s_hi = (s_int // 64).astype(BF16)
    s_lo = (s_int % 64).astype(BF16)
    kaug = jnp.stack([s_hi, s_hi, s_hi, s_lo, s_lo, s_lo], axis=1)
    kaug = jnp.pad(kaug, ((0, 0), (0, LANES - 6)))

    x2 = x.reshape(m, d)
    xb = x2.astype(BF16)

    qa = _matmul(xb, w_qa, BF16, scale=LOG2E * A_HEAD_DIM ** -0.5)
    kv = _matmul(xb, w_kv, BF16)
    qidx = _matmul(xb, w_qi, BF16)
    misc = _matmul(xb, w_misc, F32)
    lat = _matmul(xb, w_lat, F32, tn=1152)
    gates = _matmul(xb, w_gate, F32, act="sigmoid")

    tm = 512
    nst = seq // tm
    q_mla = pl.pallas_call(
        functools.partial(_mla_q_kernel, scale=LOG2E * (QK_NOPE + QK_ROPE) ** -0.5),
        grid=(m // tm,),
        in_specs=[pl.BlockSpec((tm, Q_LORA), lambda i: (i, 0)),
                  pl.BlockSpec((1, Q_LORA), lambda i: (0, 0)),
                  pl.BlockSpec((Q_LORA, B_HEADS * QK_PAD), lambda i: (0, 0)),
                  pl.BlockSpec((tm, LANES), lambda i: (i % nst, 0)),
                  pl.BlockSpec((tm, LANES), lambda i: (i % nst, 0))],
        out_specs=pl.BlockSpec((tm, B_HEADS * QK_PAD), lambda i: (i, 0)),
        out_shape=jax.ShapeDtypeStruct((m, B_HEADS * QK_PAD), BF16),
        compiler_params=_cparams(("parallel",)),
        name="mla_q_proj",
    )(lat, row(q_a_norm[0]), wq, cos2, sin2)

    k_mla, v_mla = pl.pallas_call(
        _mla_kv_kernel,
        grid=(m // tm,),
        in_specs=[pl.BlockSpec((tm, KV_LORA), lambda i: (i, 1)),
                  pl.BlockSpec((tm, LANES), lambda i: (i, (Q_LORA + KV_LORA) // LANES)),
                  pl.BlockSpec((1, KV_LORA), lambda i: (0, 0)),
                  pl.BlockSpec((KV_LORA, B_HEADS * QK_NOPE), lambda i: (0, 0)),
                  pl.BlockSpec((KV_LORA, B_HEADS * V_HEAD), lambda i: (0, 0)),
                  pl.BlockSpec((tm, LANES), lambda i: (i % nst, 0)),
                  pl.BlockSpec((tm, LANES), lambda i: (i % nst, 0))],
        out_specs=[pl.BlockSpec((tm, B_HEADS * QK_PAD), lambda i: (i, 0)),
                   pl.BlockSpec((tm, B_HEADS * V_HEAD), lambda i: (i, 0))],
        out_shape=[jax.ShapeDtypeStruct((m, B_HEADS * QK_PAD), BF16),
                   jax.ShapeDtypeStruct((m, B_HEADS * V_HEAD), BF16)],
        compiler_params=_cparams(("parallel",)),
        name="mla_kv_proj",
    )(lat, lat, row(kv_a_norm[0]), wk, wv, cos2, sin2)

    attn_b = _mla_attention(q_mla.reshape(bsz, seq, -1), k_mla.reshape(bsz, seq, -1), v_mla.reshape(bsz, seq, -1))

    k_a = kv[:, :A_HEAD_DIM].reshape(bsz, seq, A_HEAD_DIM)
    v_a = kv[:, A_HEAD_DIM:].reshape(bsz, seq, A_HEAD_DIM)
    kext = jnp.concatenate([k_a, jnp.broadcast_to(kaug[None], (bsz, seq, LANES))], axis=-1)
    k_idx = misc[:, :IDX_DIM].astype(BF16).reshape(bsz, seq, IDX_DIM)
    w_idx = misc[:, IDX_DIM:IDX_DIM + IDX_HEADS].reshape(bsz, seq, IDX_HEADS)
    attn_a = _dsa_attention(qa.reshape(bsz, seq, -1), qidx.reshape(bsz, seq, -1), w_idx, kext, v_a, k_idx, aug,
                            topk=topk)

    tmm, tnm = 1024, 512
    merged = pl.pallas_call(
        _merge_kernel,
        grid=(d // tnm, m // tmm),
        in_specs=[pl.BlockSpec((tmm, d), lambda j, i: (i, 0)),
                  pl.BlockSpec((tmm, d), lambda j, i: (i, 0)),
                  pl.BlockSpec((d, tnm), lambda j, i: (0, j)),
                  pl.BlockSpec((d, tnm), lambda j, i: (0, j)),
                  pl.BlockSpec((tmm, tnm), lambda j, i: (i, j)),
                  pl.BlockSpec((tmm, tnm), lambda j, i: (i, j + d // tnm))],
        out_specs=pl.BlockSpec((tmm, tnm), lambda j, i: (i, j)),
        out_shape=jax.ShapeDtypeStruct((m, d), BF16),
        compiler_params=_cparams(("parallel", "parallel")),
        name="branch_merge",
    )(attn_a.reshape(m, d), attn_b.reshape(m, d), w_branch_a[0].astype(BF16), w_branch_b[0].astype(BF16),
      gates, gates)

    tml = 512
    x1, x1b = pl.pallas_call(
        _proj_ln_kernel,
        grid=(m // tml,),
        in_specs=[pl.BlockSpec((tml, d), lambda i: (i, 0)),
                  pl.BlockSpec((tml, d), lambda i: (i, 0)),
                  pl.BlockSpec((d, d), lambda i: (0, 0)),
                  pl.BlockSpec((1, d), lambda i: (0, 0)),
                  pl.BlockSpec((1, d), lambda i: (0, 0))],
        out_specs=[pl.BlockSpec((tml, d), lambda i: (i, 0)), pl.BlockSpec((tml, d), lambda i: (i, 0))],
        out_shape=[jax.ShapeDtypeStruct((m, d), F32), jax.ShapeDtypeStruct((m, d), BF16)],
        compiler_params=_cparams(("parallel",)),
        name="oproj_ln1",
    )(x2, merged, w_o[0].astype(BF16), row(ln1_g[0]), row(ln1_b[0]))

    act = _ffn_up(x1b, w_up[0].astype(BF16), conv_w[0], row(conv_b[0]), seq)

    tnp = 1024
    ple = pl.pallas_call(
        _ple_kernel,
        grid=(d // tnp, m // tmm),
        in_specs=[pl.BlockSpec((tmm, d), lambda j, i: (i, 0)),
                  pl.BlockSpec((tmm, PLE_DIM), lambda j, i: (i, 0)),
                  pl.BlockSpec((d, tnp), lambda j, i: (0, j)),
                  pl.BlockSpec((1, tnp), lambda j, i: (0, j)),
                  pl.BlockSpec((PLE_DIM, tnp), lambda j, i: (0, j))],
        out_specs=pl.BlockSpec((tmm, tnp), lambda j, i: (i, j)),
        out_shape=jax.ShapeDtypeStruct((m, d), F32),
        compiler_params=_cparams(("parallel", "parallel")),
        name="ple_gate",
    )(x1b, p[0].reshape(m, PLE_DIM).astype(BF16), w_pg[0].astype(BF16), row(b_pg[0]), w_pe[0].astype(BF16))

    out = _ffn_down(x1, act, w_down[0].astype(BF16), ple, row(ln2_g[0]), row(ln2_b[0]))
    return out.reshape(bsz, seq, d)
```

```python
import functools

import jax
import jax.numpy as jnp
import numpy as np
from jax import lax
from jax.experimental import pallas as pl
from jax.experimental.pallas import tpu as pltpu

F32 = jnp.float32
BF16 = jnp.bfloat16

D_MODEL = 2048
A_HEADS = 16
A_HEAD_DIM = 128
IDX_HEADS = 16
IDX_DIM = 64
INDEX_TOPK_MAX = 256
B_HEADS = 16
Q_LORA = 512
KV_LORA = 512
QK_NOPE = 128
QK_ROPE = 64
V_HEAD = 128
ROPE_THETA = 10000.0
D_FF = 5632
CONV_W = 3
PLE_DIM = 256
LN_EPS = 1e-5
RMS_EPS = 1e-6
NEG = -1e30
DEEPNORM_ALPHA = 2.0 ** 0.25
LOG2E = 1.4426950408889634

LANES = 128
QK_PAD = 256
VE = 2 * LANES
VMEM_LIMIT = 56 * 1024 * 1024
INT_MIN = -2 ** 31
INT_MAX = 2 ** 31 - 1


def _cparams(sem, flags=None):
    return pltpu.CompilerParams(dimension_semantics=sem, vmem_limit_bytes=VMEM_LIMIT, flags=flags)


def _dot(a, b):
    return jnp.dot(a, b, preferred_element_type=F32)


def _dot_nt(a, b):
    return lax.dot_general(a, b, (((1,), (1,)), ((), ())), preferred_element_type=F32)


def _layer_norm(y, g, b):
    mu = jnp.mean(y, axis=-1, keepdims=True)
    d = y - mu
    var = jnp.mean(d * d, axis=-1, keepdims=True)
    return d * lax.rsqrt(var + LN_EPS) * g + b


def _rms_norm(c, g):
    return c * lax.rsqrt(jnp.mean(c * c, axis=-1, keepdims=True) + RMS_EPS) * g


def _rope_lanes(pe, cos2, sin2):
    lane = lax.broadcasted_iota(jnp.int32, pe.shape, 1)
    swapped = jnp.where(lane < QK_ROPE // 2, pltpu.roll(pe, LANES - QK_ROPE // 2, 1),
                        pltpu.roll(pe, QK_ROPE // 2, 1))
    return pe * cos2 + swapped * sin2


def _mm_kernel(x_ref, w_ref, o_ref, *, scale, act):
    y = _dot(x_ref[...], w_ref[...])
    if scale != 1.0:
        y = y * scale
    if act == "sigmoid":
        y = jax.nn.sigmoid(y)
    o_ref[...] = y.astype(o_ref.dtype)


def _matmul(x, w, out_dtype, *, scale=1.0, act=None, tm=1024, tn=1024):
    m, k = x.shape
    n = w.shape[1]
    tn = min(tn, n)
    assert m % tm == 0 and n % tn == 0
    return pl.pallas_call(
        functools.partial(_mm_kernel, scale=scale, act=act),
        grid=(n // tn, m // tm),
        in_specs=[pl.BlockSpec((tm, k), lambda j, i: (i, 0)),
                  pl.BlockSpec((k, tn), lambda j, i: (0, j))],
        out_specs=pl.BlockSpec((tm, tn), lambda j, i: (i, j)),
        out_shape=jax.ShapeDtypeStruct((m, n), out_dtype),
        compiler_params=_cparams(("parallel", "parallel")),
        name="proj_mm",
    )(x, w)


def _mla_q_kernel(c_ref, g_ref, w_ref, cos_ref, sin_ref, o_ref, *, scale):
    y = _rms_norm(c_ref[...], g_ref[...]).astype(BF16)
    q = _dot(y, w_ref[...])
    cos2 = cos_ref[...]
    sin2 = sin_ref[...]
    for h in range(B_HEADS):
        base = h * QK_PAD
        o_ref[:, base:base + QK_NOPE] = (q[:, base:base + QK_NOPE] * scale).astype(BF16)
        pe = q[:, base + QK_NOPE:base + QK_PAD]
        o_ref[:, base + QK_NOPE:base + QK_PAD] = (_rope_lanes(pe, cos2, sin2) * scale).astype(BF16)


def _mla_kv_kernel(c_ref, kpe_ref, g_ref, wk_ref, wv_ref, cos_ref, sin_ref, k_ref, v_ref):
    y = _rms_norm(c_ref[...], g_ref[...]).astype(BF16)
    kn = _dot(y, wk_ref[...])
    v_ref[...] = _dot(y, wv_ref[...]).astype(BF16)
    kpe = _rope_lanes(kpe_ref[...], cos_ref[...], sin_ref[...]).astype(BF16)
    for h in range(B_HEADS):
        base = h * QK_PAD
        k_ref[:, base:base + QK_NOPE] = kn[:, h * QK_NOPE:(h + 1) * QK_NOPE].astype(BF16)
        k_ref[:, base + QK_NOPE:base + QK_PAD] = kpe


def _flash_update(s, v_chunk, m_ref, l_ref, acc_ref, rows):
    nblk = s.shape[1] // LANES
    m_prev = m_ref[rows, :]
    m_new = jnp.maximum(m_prev, jnp.max(s, axis=-1, keepdims=True))
    alpha = jnp.exp2(m_prev - m_new)
    ps = [jnp.exp2(s[:, j * LANES:(j + 1) * LANES] - m_new) for j in range(nblk)]
    p = jnp.concatenate([x.astype(BF16) for x in ps], axis=1)
    if l_ref is None:
        acc_ref[rows, :] = jnp.concatenate([alpha, alpha], axis=1) * acc_ref[rows, :] + _dot(p, v_chunk)
    else:
        l_ref[rows, :] = alpha * l_ref[rows, :] + functools.reduce(lambda a, b: a + b, ps)
        acc_ref[rows, :] = alpha * acc_ref[rows, :] + _dot(p, v_chunk)
    m_ref[rows, :] = m_new


def _flash_finish(l_ref, acc_ref, rows):
    if l_ref is None:
        return acc_ref[rows, 0:LANES] / acc_ref[rows, LANES:VE]
    return acc_ref[rows, :] / jnp.sum(l_ref[rows, :], axis=-1, keepdims=True)


def _mla_attn_kernel(q_ref, k_ref, v_ref, o_ref, m_ref, l_ref, acc_ref, *, tq, tk, td, hp):
    qi = pl.program_id(2)
    m_ref[...] = jnp.full(m_ref.shape, NEG, F32)
    l_ref[...] = jnp.zeros(l_ref.shape, F32)
    acc_ref[...] = jnp.zeros(acc_ref.shape, F32)

    def tile(off, width, r0, diag):
        for h in range(hp):
            q = q_ref[0, r0:tq, h * QK_PAD:(h + 1) * QK_PAD]
            s = _dot_nt(q, k_ref[0, pl.ds(off, width), h * QK_PAD:(h + 1) * QK_PAD])
            if diag is not None:
                row = lax.broadcasted_iota(jnp.int32, s.shape, 0) + r0
                col = lax.broadcasted_iota(jnp.int32, s.shape, 1) + diag
                s = jnp.where(col <= row, s, NEG)
            _flash_update(s, v_ref[0, pl.ds(off, width), h * V_HEAD:(h + 1) * V_HEAD],
                          m_ref, l_ref, acc_ref, slice(h * tq + r0, (h + 1) * tq))

    def body(c, carry):
        tile(pl.multiple_of(c * tk, tk), tk, 0, None)
        return carry

    lax.fori_loop(0, qi * (tq // tk), body, 0)
    for d in range(tq // td):
        tile(pl.multiple_of(qi * tq + d * td, td), td, d * td, d * td)
    for h in range(hp):
        o_ref[0, :, h * V_HEAD:(h + 1) * V_HEAD] = _flash_finish(
            l_ref, acc_ref, slice(h * tq, (h + 1) * tq)).astype(o_ref.dtype)


def _monotone_key(x):
    bits = lax.bitcast_convert_type(x, jnp.int32)
    return bits ^ ((bits >> 31) & INT_MAX)


def _dsa_kernel(qa_ref, qi_ref, w_ref, kext_ref, v_ref, ke_ref, ko_ref, aug_ref, o_ref,
                key_ref, lhs_ref, qs_ref, wb_ref, m_ref, l_ref, acc_ref,
                *, q_blk, topk, chunk, group, ones, blind):
    if ones:
        l_ref = None
    qb = pl.program_id(1)
    n_chunks = (qb * q_blk + q_blk + chunk - 1) // chunk
    rep = chunk // LANES
    n_pairs = IDX_HEADS // 2

    for h in range(A_HEADS):
        lhs_ref[h * q_blk:(h + 1) * q_blk, 0:A_HEAD_DIM] = qa_ref[0, :, h * A_HEAD_DIM:(h + 1) * A_HEAD_DIM]
    lhs_ref[:, A_HEAD_DIM:2 * A_HEAD_DIM] = aug_ref[...]
    for j in range(n_pairs):
        qs_ref[j * q_blk:(j + 1) * q_blk, :] = qi_ref[0, :, j * LANES:(j + 1) * LANES]
    for h in range(IDX_HEADS):
        wb_ref[h] = jnp.broadcast_to(w_ref[0, :, h:h + 1], (q_blk, LANES))

    t_col = qb * q_blk + lax.broadcasted_iota(jnp.int32, (q_blk, 1), 0)

    def score_chunk(c, carry):
        off = pl.multiple_of(c * chunk, chunk)
        qs = qs_ref[...]
        se = _dot_nt(qs, ke_ref[0, pl.ds(off, chunk), :])
        so = _dot_nt(qs, ko_ref[0, pl.ds(off, chunk), :])
        acc = jnp.zeros((q_blk, chunk), F32)
        for j in range(n_pairs):
            we = jnp.concatenate([wb_ref[2 * j]] * rep, axis=1)
            wo = jnp.concatenate([wb_ref[2 * j + 1]] * rep, axis=1)
            acc = acc + we * jnp.maximum(se[j * q_blk:(j + 1) * q_blk], 0.0)
            acc = acc + wo * jnp.maximum(so[j * q_blk:(j + 1) * q_blk], 0.0)
        s_pos = off + lax.broadcasted_iota(jnp.int32, (q_blk, chunk), 1)
        key_ref[:, pl.ds(off, chunk)] = jnp.where(s_pos <= t_col, _monotone_key(acc), INT_MIN)
        return carry

    lax.fori_loop(0, n_chunks, score_chunk, 0)

    kf = float(topk)

    def count_ge(th):
        thb = jnp.broadcast_to(th, (q_blk, LANES))

        def body(c, cnt):
            off = pl.multiple_of(c * chunk, chunk)
            blk = key_ref[:, pl.ds(off, chunk)]
            for j in range(rep):
                cnt = cnt + jnp.where(blk[:, j * LANES:(j + 1) * LANES] >= thb, 1.0, 0.0)
            return cnt

        cnt = lax.fori_loop(0, n_chunks, body, jnp.zeros((q_blk, LANES), F32))
        return jnp.sum(cnt, axis=-1, keepdims=True)

    n_causal = (t_col + 1).astype(F32)
    few = n_causal <= kf
    lo0 = jnp.full((q_blk, 1), INT_MIN + 1, jnp.int32)
    hi0 = jnp.where(few, INT_MIN + 2, INT_MAX).astype(jnp.int32)
    cnt_lo0 = n_causal
    cnt_hi0 = jnp.zeros((q_blk, 1), F32)

    def active(lo, hi, cnt_lo):
        return jnp.logical_and(lo + 1 < hi, cnt_lo != kf)

    def bis_step(st):
        lo, hi, cnt_lo, cnt_hi = st
        act = active(lo, hi, cnt_lo)
        mid = (lo & hi) + ((lo ^ hi) >> 1)
        c = count_ge(mid)
        ge = jnp.logical_and(act, c >= kf)
        lt = jnp.logical_and(act, c < kf)
        return (jnp.where(ge, mid, lo), jnp.where(lt, mid, hi), jnp.where(ge, c, cnt_lo), jnp.where(lt, c, cnt_hi))

    def bis_cond(st):
        it, lo, hi, cnt_lo, cnt_hi = st
        n_act = jnp.max(jnp.where(active(lo, hi, cnt_lo), 1, 0))
        return jnp.logical_and(it < 32 - blind + 2, n_act > 0)

    st = lax.fori_loop(0, blind, lambda _, s_: bis_step(s_), (lo0, hi0, cnt_lo0, cnt_hi0))
    _, v_key, _, cnt_lo, cnt_hi = lax.while_loop(
        bis_cond, lambda s_: (s_[0] + 1,) + bis_step(s_[1:]), (jnp.int32(0),) + st)

    tied = jnp.logical_and(cnt_lo > kf, jnp.logical_not(few))
    need = kf - cnt_hi
    n_tied = jnp.max(jnp.where(tied, 1, 0))

    def count_tie_le(jj):
        vb = jnp.broadcast_to(v_key, (q_blk, LANES))
        jb = jnp.broadcast_to(jj, (q_blk, LANES))
        lane = lax.broadcasted_iota(jnp.int32, (q_blk, LANES), 1)

        def body(c, cnt):
            off = pl.multiple_of(c * chunk, chunk)
            blk = key_ref[:, pl.ds(off, chunk)]
            for j in range(rep):
                in_range = jnp.where(off + j * LANES + lane <= jb, 1.0, 0.0)
                cnt = cnt + jnp.where(blk[:, j * LANES:(j + 1) * LANES] == vb, in_range, 0.0)
            return cnt

        cnt = lax.fori_loop(0, n_chunks, body, jnp.zeros((q_blk, LANES), F32))
        return jnp.sum(cnt, axis=-1, keepdims=True)

    def tie_search(_):
        def body(_, st):
            lo, hi = st
            mid = (lo + hi) >> 1
            ok = count_tie_le(mid) >= need
            upd = lo + 1 < hi
            return (jnp.where(jnp.logical_and(upd, jnp.logical_not(ok)), mid, lo),
                    jnp.where(jnp.logical_and(upd, ok), mid, hi))

        lo = jnp.full((q_blk, 1), -1, jnp.int32)
        hi = jnp.broadcast_to(t_col, (q_blk, 1))
        _, hi = lax.fori_loop(0, 14, body, (lo, hi))
        return jnp.where(tied, hi, INT_MAX)

    j_max = lax.cond(n_tied > 0, tie_search, lambda _: jnp.full((q_blk, 1), INT_MAX, jnp.int32), 0)

    m_ref[...] = jnp.full(m_ref.shape, NEG, F32)
    acc_ref[...] = jnp.zeros(acc_ref.shape, F32)
    if l_ref is not None:
        l_ref[...] = jnp.zeros(l_ref.shape, F32)
    rows = group * q_blk
    n_groups = A_HEADS // group

    def attn_chunk(c, carry):
        off = pl.multiple_of(c * chunk, chunk)
        keys = key_ref[:, pl.ds(off, chunk)]
        s_pos = off + lax.broadcasted_iota(jnp.int32, (q_blk, chunk), 1)
        sel = jnp.where(keys > v_key, 1, jnp.where(keys == v_key, jnp.where(s_pos <= j_max, 1, 0), 0))
        bias = jnp.where(sel > 0, 0.0, NEG)
        kc = kext_ref[0, pl.ds(off, chunk), :]
        vc = v_ref[0, pl.ds(off, chunk), :]
        for g in range(n_groups):
            r0 = g * rows
            s = _dot_nt(lhs_ref[r0:r0 + rows, :], kc)
            s = (s.reshape(group, q_blk, chunk) + bias[None]).reshape(rows, chunk)
            _flash_update(s, vc, m_ref, l_ref, acc_ref, slice(r0, r0 + rows))
        return carry

    lax.fori_loop(0, n_chunks, attn_chunk, 0)
    for h in range(A_HEADS):
        r0 = h * q_blk
        o_ref[0, :, h * A_HEAD_DIM:(h + 1) * A_HEAD_DIM] = _flash_finish(
            l_ref, acc_ref, slice(r0, r0 + q_blk)).astype(o_ref.dtype)


def _merge_kernel(a_ref, b_ref, wa_ref, wb_ref, g0_ref, g1_ref, o_ref):
    ya = _dot(a_ref[...], wa_ref[...])
    yb = _dot(b_ref[...], wb_ref[...])
    o_ref[...] = (g0_ref[...] * ya + g1_ref[...] * yb).astype(o_ref.dtype)


def _proj_ln_kernel(x_ref, a_ref, w_ref, g_ref, b_ref, o_ref, obf_ref):
    y = DEEPNORM_ALPHA * x_ref[...] + _dot(a_ref[...], w_ref[...])
    out = _layer_norm(y, g_ref[...], b_ref[...])
    o_ref[...] = out
    obf_ref[...] = out.astype(BF16)


def _ffn_up_kernel(x_ref, wg_ref, wv_ref, cwg_ref, cwv_ref, cbg_ref, cbv_ref, o_ref, tail_g_ref, tail_v_ref,
                   *, tm, seq):
    i = pl.program_id(1)

    @pl.when((i * tm) % seq == 0)
    def _():
        tail_g_ref[...] = jnp.zeros(tail_g_ref.shape, F32)
        tail_v_ref[...] = jnp.zeros(tail_v_ref.shape, F32)

    x = x_ref[...]
    row = lax.broadcasted_iota(jnp.int32, (tm, 1), 0)

    def conv(w_ref, cw_ref, cb_ref, tail_ref):
        h = _dot(x, w_ref[...])
        hp = tail_ref[...]
        tail_ref[...] = h[tm - 8:tm, :]
        hm1 = jnp.where(row == 0, hp[7:8, :], pltpu.roll(h, 1, 0))
        hm2 = jnp.where(row == 0, hp[6:7, :], jnp.where(row == 1, hp[7:8, :], pltpu.roll(h, 2, 0)))
        return cw_ref[0:1, :] * hm2 + cw_ref[1:2, :] * hm1 + cw_ref[2:3, :] * h + cb_ref[...]

    hg = conv(wg_ref, cwg_ref, cbg_ref, tail_g_ref)
    hv = conv(wv_ref, cwv_ref, cbv_ref, tail_v_ref)
    o_ref[...] = (hg * jax.nn.sigmoid(hg) * hv).astype(o_ref.dtype)


def _ple_kernel(x_ref, p_ref, wpg_ref, bpg_ref, wpe_ref, o_ref):
    gate = jax.nn.sigmoid(_dot(x_ref[...], wpg_ref[...]) + bpg_ref[...])
    o_ref[...] = gate * _dot(p_ref[...], wpe_ref[...])


def _down_ln_kernel(x_ref, a_ref, w_ref, ple_ref, g_ref, b_ref, o_ref, acc_ref):
    k = pl.program_id(1)

    @pl.when(k == 0)
    def _():
        acc_ref[...] = DEEPNORM_ALPHA * x_ref[...] + ple_ref[...]

    acc_ref[...] += _dot(a_ref[...], w_ref[...])

    @pl.when(k == pl.num_programs(1) - 1)
    def _():
        o_ref[...] = _layer_norm(acc_ref[...], g_ref[...], b_ref[...])


def _mla_attention(q, k, v, *, tq=1024, tk=1024, td=256, hp=4, flags=None):
    bsz, seq, _ = q.shape
    assert tq % tk == 0 and tq % td == 0
    return pl.pallas_call(
        functools.partial(_mla_attn_kernel, tq=tq, tk=tk, td=td, hp=hp),
        grid=(bsz, B_HEADS // hp, seq // tq),
        in_specs=[pl.BlockSpec((1, tq, hp * QK_PAD), lambda b, h, i: (b, i, h)),
                  pl.BlockSpec((1, seq, hp * QK_PAD), lambda b, h, i: (b, 0, h), pipeline_mode=pl.Buffered(1)),
                  pl.BlockSpec((1, seq, hp * V_HEAD), lambda b, h, i: (b, 0, h), pipeline_mode=pl.Buffered(1))],
        out_specs=pl.BlockSpec((1, tq, hp * V_HEAD), lambda b, h, i: (b, i, h)),
        out_shape=jax.ShapeDtypeStruct((bsz, seq, B_HEADS * V_HEAD), BF16),
        scratch_shapes=[pltpu.VMEM((hp * tq, LANES), F32), pltpu.VMEM((hp * tq, LANES), F32),
                        pltpu.VMEM((hp * tq, V_HEAD), F32)],
        compiler_params=_cparams(("parallel", "parallel", "arbitrary"), flags),
        name="mla_attn",
    )(q, k, v)


def _dsa_attention(qa, qidx, w_idx, kext, v_a, k_idx, aug, *, topk, q_blk=128, chunk=1024, group=8, ones=False,
                   blind=20, flags=None):
    bsz, seq, _ = qa.shape
    zk = jnp.zeros_like(k_idx)
    k_even = jnp.concatenate([k_idx, zk], axis=-1)
    k_odd = jnp.concatenate([zk, k_idx], axis=-1)
    aug_rows = jnp.repeat(aug, q_blk, axis=0)
    v_in = jnp.concatenate([v_a, jnp.ones_like(v_a)], axis=-1) if ones else v_a
    vw = v_in.shape[-1]
    return pl.pallas_call(
        functools.partial(_dsa_kernel, q_blk=q_blk, topk=topk, chunk=chunk, group=group, ones=ones, blind=blind),
        grid=(bsz, seq // q_blk),
        in_specs=[pl.BlockSpec((1, q_blk, A_HEADS * A_HEAD_DIM), lambda b, i: (b, i, 0)),
                  pl.BlockSpec((1, q_blk, IDX_HEADS * IDX_DIM), lambda b, i: (b, i, 0)),
                  pl.BlockSpec((1, q_blk, IDX_HEADS), lambda b, i: (b, i, 0)),
                  pl.BlockSpec((1, seq, 2 * A_HEAD_DIM), lambda b, i: (b, 0, 0)),
                  pl.BlockSpec((1, seq, vw), lambda b, i: (b, 0, 0)),
                  pl.BlockSpec((1, seq, LANES), lambda b, i: (b, 0, 0)),
                  pl.BlockSpec((1, seq, LANES), lambda b, i: (b, 0, 0)),
                  pl.BlockSpec((A_HEADS * q_blk, LANES), lambda b, i: (0, 0))],
        out_specs=pl.BlockSpec((1, q_blk, A_HEADS * A_HEAD_DIM), lambda b, i: (b, i, 0)),
        out_shape=jax.ShapeDtypeStruct((bsz, seq, A_HEADS * A_HEAD_DIM), BF16),
        scratch_shapes=[pltpu.VMEM((q_blk, seq), jnp.int32),
                        pltpu.VMEM((A_HEADS * q_blk, 2 * A_HEAD_DIM), BF16),
                        pltpu.VMEM((IDX_HEADS // 2 * q_blk, LANES), BF16),
                        pltpu.VMEM((IDX_HEADS, q_blk, LANES), F32),
                        pltpu.VMEM((A_HEADS * q_blk, LANES), F32),
                        pltpu.VMEM((8, LANES) if ones else (A_HEADS * q_blk, LANES), F32),
                        pltpu.VMEM((A_HEADS * q_blk, vw), F32)],
        compiler_params=_cparams(("parallel", "arbitrary"), flags),
        name="dsa_attn",
    )(qa, qidx, w_idx, kext, v_in, k_even, k_odd, aug_rows)


def _ffn_up(x1b, w_up_b, cw, cb, seq, *, tm=1024, tn=512):
    m, d = x1b.shape
    nft = D_FF // tn
    return pl.pallas_call(
        functools.partial(_ffn_up_kernel, tm=tm, seq=seq),
        grid=(nft, m // tm),
        in_specs=[pl.BlockSpec((tm, d), lambda j, i: (i, 0)),
                  pl.BlockSpec((d, tn), lambda j, i: (0, j)),
                  pl.BlockSpec((d, tn), lambda j, i: (0, j + nft)),
                  pl.BlockSpec((CONV_W, tn), lambda j, i: (0, j)),
                  pl.BlockSpec((CONV_W, tn), lambda j, i: (0, j + nft)),
                  pl.BlockSpec((1, tn), lambda j, i: (0, j)),
                  pl.BlockSpec((1, tn), lambda j, i: (0, j + nft))],
        out_specs=pl.BlockSpec((tm, tn), lambda j, i: (i, j)),
        out_shape=jax.ShapeDtypeStruct((m, D_FF), BF16),
        scratch_shapes=[pltpu.VMEM((8, tn), F32), pltpu.VMEM((8, tn), F32)],
        compiler_params=_cparams(("parallel", "arbitrary")),
        name="ffn_up_conv",
    )(x1b, w_up_b, w_up_b, cw, cw, cb, cb)


def _ffn_down(x1, act, w_down_b, ple, g, b, *, tm=256, tk=D_FF):
    m, d = x1.shape
    return pl.pallas_call(
        _down_ln_kernel,
        grid=(m // tm, D_FF // tk),
        in_specs=[pl.BlockSpec((tm, d), lambda i, k: (i, 0)),
                  pl.BlockSpec((tm, tk), lambda i, k: (i, k)),
                  pl.BlockSpec((tk, d), lambda i, k: (k, 0)),
                  pl.BlockSpec((tm, d), lambda i, k: (i, 0)),
                  pl.BlockSpec((1, d), lambda i, k: (0, 0)),
                  pl.BlockSpec((1, d), lambda i, k: (0, 0))],
        out_specs=pl.BlockSpec((tm, d), lambda i, k: (i, 0)),
        out_shape=jax.ShapeDtypeStruct((m, d), F32),
        scratch_shapes=[pltpu.VMEM((tm, d), F32)],
        compiler_params=_cparams(("parallel", "arbitrary")),
        name="ffn_down_ln2",
    )(x1, act, w_down_b, ple, g, b)


def _split_bf16(x, parts):
    out = []
    rem = x
    for _ in range(parts):
        piece = rem.astype(BF16)
        out.append(piece)
        rem = rem - piece.astype(F32)
    return out


def kernel(x, p, w_in, q_a_norm, kv_a_norm, w_uq, w_uk, w_uv, w_branch_a, w_branch_b, w_o,
           ln1_g, ln1_b, w_up, conv_w, conv_b, w_down, w_pg, b_pg, w_pe, ln2_g, ln2_b):
    bsz, seq, d = x.shape
    m = bsz * seq
    topk = min(INDEX_TOPK_MAX, seq // 4)
    row = lambda a: a.reshape(1, -1)

    wi = w_in[0]
    o_qa, o_ka, o_va = 0, 2048, 2176
    o_qi, o_ki, o_wi, o_cq, o_ckv, o_kpe, o_gate = 2304, 3328, 3392, 3408, 3920, 4432, 4496
    w_qa = wi[:, o_qa:o_ka].astype(BF16)
    w_kv = wi[:, o_ka:o_qi].astype(BF16)
    w_qi = wi[:, o_qi:o_ki].astype(BF16)
    w_misc = jnp.pad(wi[:, o_ki:o_cq], ((0, 0), (0, LANES - (o_cq - o_ki)))).astype(BF16)
    w_lat = jnp.pad(wi[:, o_cq:o_gate], ((0, 0), (0, LANES - QK_ROPE))).astype(BF16)
    w_gate = wi[:, o_gate:].astype(BF16)

    wq = jnp.pad(w_uq[0], ((0, 0), (0, 0), (0, QK_PAD - QK_NOPE - QK_ROPE)))
    wq = wq.reshape(Q_LORA, B_HEADS * QK_PAD).astype(BF16)
    wk = w_uk[0].reshape(KV_LORA, B_HEADS * QK_NOPE).astype(BF16)
    wv = w_uv[0].reshape(KV_LORA, B_HEADS * V_HEAD).astype(BF16)

    pos = jnp.arange(seq, dtype=F32)
    inv_freq = jnp.power(ROPE_THETA, -jnp.arange(0, QK_ROPE, 2, dtype=F32) / QK_ROPE)
    ang = pos[:, None] * inv_freq[None, :]
    cos, sin = jnp.cos(ang), jnp.sin(ang)
    zpad = jnp.zeros((seq, LANES - QK_ROPE), F32)
    cos2 = jnp.concatenate([cos, cos, zpad], axis=1)
    sin2 = jnp.concatenate([-sin, sin, zpad], axis=1)

    slopes = LOG2E * jnp.exp2(-8.0 * jnp.arange(1, A_HEADS + 1, dtype=F32) / A_HEADS)
    sl = _split_bf16(slopes, 3)
    aug = jnp.stack([s_ * 64.0 for s_ in sl] + sl, axis=1).astype(BF16)
    aug = jnp.pad(aug, ((0, 0), (0, LANES - 6)))
    s_int = jnp.arange(seq, dtype=jnp.int32)
    s_hi = (s_int // 64).astype(BF16)
    s_lo = (s_int % 64).astype(BF16)
    kaug = jnp.stack([s_hi, s_hi, s_hi, s_lo, s_lo, s_lo], axis=1)
    kaug = jnp.pad(kaug, ((0, 0), (0, LANES - 6)))

    x2 = x.reshape(m, d)
    xb = x2.astype(BF16)

    qa = _matmul(xb, w_qa, BF16, scale=LOG2E * A_HEAD_DIM ** -0.5)
    kv = _matmul(xb, w_kv, BF16)
    qidx = _matmul(xb, w_qi, BF16)
    misc = _matmul(xb, w_misc, F32)
    lat = _matmul(xb, w_lat, F32, tn=1152)
    gates = _matmul(xb, w_gate, F32, act="sigmoid")

    tm = 512
    nst = seq // tm
    q_mla = pl.pallas_call(
        functools.partial(_mla_q_kernel, scale=LOG2E * (QK_NOPE + QK_ROPE) ** -0.5),
        grid=(m // tm,),
        in_specs=[pl.BlockSpec((tm, Q_LORA), lambda i: (i, 0)),
                  pl.BlockSpec((1, Q_LORA), lambda i: (0, 0)),
                  pl.BlockSpec((Q_LORA, B_HEADS * QK_PAD), lambda i: (0, 0)),
                  pl.BlockSpec((tm, LANES), lambda i: (i % nst, 0)),
                  pl.BlockSpec((tm, LANES), lambda i: (i % nst, 0))],
        out_specs=pl.BlockSpec((tm, B_HEADS * QK_PAD), lambda i: (i, 0)),
        out_shape=jax.ShapeDtypeStruct((m, B_HEADS * QK_PAD), BF16),
        compiler_params=_cparams(("parallel",)),
        name="mla_q_proj",
    )(lat, row(q_a_norm[0]), wq, cos2, sin2)

    k_mla, v_mla = pl.pallas_call(
        _mla_kv_kernel,
        grid=(m // tm,),
        in_specs=[pl.BlockSpec((tm, KV_LORA), lambda i: (i, 1)),
                  pl.BlockSpec((tm, LANES), lambda i: (i, (Q_LORA + KV_LORA) // LANES)),
                  pl.BlockSpec((1, KV_LORA), lambda i: (0, 0)),
                  pl.BlockSpec((KV_LORA, B_HEADS * QK_NOPE), lambda i: (0, 0)),
                  pl.BlockSpec((KV_LORA, B_HEADS * V_HEAD), lambda i: (0, 0)),
                  pl.BlockSpec((tm, LANES), lambda i: (i % nst, 0)),
                  pl.BlockSpec((tm, LANES), lambda i: (i % nst, 0))],
        out_specs=[pl.BlockSpec((tm, B_HEADS * QK_PAD), lambda i: (i, 0)),
                   pl.BlockSpec((tm, B_HEADS * V_HEAD), lambda i: (i, 0))],
        out_shape=[jax.ShapeDtypeStruct((m, B_HEADS * QK_PAD), BF16),
                   jax.ShapeDtypeStruct((m, B_HEADS * V_HEAD), BF16)],
        compiler_params=_cparams(("parallel",)),
        name="mla_kv_proj",
    )(lat, lat, row(kv_a_norm[0]), wk, wv, cos2, sin2)

    attn_b = _mla_attention(q_mla.reshape(bsz, seq, -1), k_mla.reshape(bsz, seq, -1), v_mla.reshape(bsz, seq, -1))

    k_a = kv[:, :A_HEAD_DIM].reshape(bsz, seq, A_HEAD_DIM)
    v_a = kv[:, A_HEAD_DIM:].reshape(bsz, seq, A_HEAD_DIM)
    kext = jnp.concatenate([k_a, jnp.broadcast_to(kaug[None], (bsz, seq, LANES))], axis=-1)
    k_idx = misc[:, :IDX_DIM].astype(BF16).reshape(bsz, seq, IDX_DIM)
    w_idx = misc[:, IDX_DIM:IDX_DIM + IDX_HEADS].reshape(bsz, seq, IDX_HEADS)
    attn_a = _dsa_attention(qa.reshape(bsz, seq, -1), qidx.reshape(bsz, seq, -1), w_idx, kext, v_a, k_idx, aug,
                            topk=topk)

    tmm, tnm = 1024, 512
    merged = pl.pallas_call(
        _merge_kernel,
        grid=(d // tnm, m // tmm),
        in_specs=[pl.BlockSpec((tmm, d), lambda j, i: (i, 0)),
                  pl.BlockSpec((tmm, d), lambda j, i: (i, 0)),
                  pl.BlockSpec((d, tnm), lambda j, i: (0, j)),
                  pl.BlockSpec((d, tnm), lambda j, i: (0, j)),
                  pl.BlockSpec((tmm, tnm), lambda j, i: (i, j)),
                  pl.BlockSpec((tmm, tnm), lambda j, i: (i, j + d // tnm))],
        out_specs=pl.BlockSpec((tmm, tnm), lambda j, i: (i, j)),
        out_shape=jax.ShapeDtypeStruct((m, d), BF16),
        compiler_params=_cparams(("parallel", "parallel")),
        name="branch_merge",
    )(attn_a.reshape(m, d), attn_b.reshape(m, d), w_branch_a[0].astype(BF16), w_branch_b[0].astype(BF16),
      gates, gates)

    tml = 512
    x1, x1b = pl.pallas_call(
        _proj_ln_kernel,
        grid=(m // tml,),
        in_specs=[pl.BlockSpec((tml, d), lambda i: (i, 0)),
                  pl.BlockSpec((tml, d), lambda i: (i, 0)),
                  pl.BlockSpec((d, d), lambda i: (0, 0)),
                  pl.BlockSpec((1, d), lambda i: (0, 0)),
                  pl.BlockSpec((1, d), lambda i: (0, 0))],
        out_specs=[pl.BlockSpec((tml, d), lambda i: (i, 0)), pl.BlockSpec((tml, d), lambda i: (i, 0))],
        out_shape=[jax.ShapeDtypeStruct((m, d), F32), jax.ShapeDtypeStruct((m, d), BF16)],
        compiler_params=_cparams(("parallel",)),
        name="oproj_ln1",
    )(x2, merged, w_o[0].astype(BF16), row(ln1_g[0]), row(ln1_b[0]))

    act = _ffn_up(x1b, w_up[0].astype(BF16), conv_w[0], row(conv_b[0]), seq)

    tnp = 1024
    ple = pl.pallas_call(
        _ple_kernel,
        grid=(d // tnp, m // tmm),
        in_specs=[pl.BlockSpec((tmm, d), lambda j, i: (i, 0)),
                  pl.BlockSpec((tmm, PLE_DIM), lambda j, i: (i, 0)),
                  pl.BlockSpec((d, tnp), lambda j, i: (0, j)),
                  pl.BlockSpec((1, tnp), lambda j, i: (0, j)),
                  pl.BlockSpec((PLE_DIM, tnp), lambda j, i: (0, j))],
        out_specs=pl.BlockSpec((tmm, tnp), lambda j, i: (i, j)),
        out_shape=jax.ShapeDtypeStruct((m, d), F32),
        compiler_params=_cparams(("parallel", "parallel")),
        name="ple_gate",
    )(x1b, p[0].reshape(m, PLE_DIM).astype(BF16), w_pg[0].astype(BF16), row(b_pg[0]), w_pe[0].astype(BF16))

    out = _ffn_down(x1, act, w_down[0].astype(BF16), ple, row(ln2_g[0]), row(ln2_b[0]))
    return out.reshape(bsz, seq, d)
```
